```python
import math
import jax, jax.numpy as jnp
from jax import lax
import numpy as np

D_MODEL = 2048
BATCH = 2
SEQ = 4096
DEPTH = 4
DEC_BATCH = 128
DEC_SEQ = 1
PAST_LEN = 8192
PAGE_SIZE = 128

H_A = 8
KV_A = 2
DH_A = 64
H_B = 8
Q_LORA = 512
KV_LORA = 256
NOPE_B = 128
ROPE_B = 64
V_B = 128
H_C = 32
KV_C = 4
HD_C = 64
WINDOW = 128
D_FF = 4 * D_MODEL

ROPE_THETA = 500000.0
ROT_FRAC = 4
Q_BLOCK = 128
EPS = 1e-6
NEG_INF = -1e30

N_AB = (DEPTH + 1) // 2
N_C = DEPTH // 2
IN_AB = H_A * 2 * DH_A + 2 * KV_A * 2 * DH_A + Q_LORA + KV_LORA + ROPE_B
MIX_AB = H_A * 2 * DH_A + H_B * V_B
IN_C = (H_C + 2 * KV_C) * HD_C

kernel_name = 'hybrid_diffattn_mla_swa_sink_decoder_step'


def _rms(x, g):
    xf = x.astype(jnp.float32)
    y = xf * lax.rsqrt(jnp.mean(xf * xf, axis=-1, keepdims=True) + EPS)
    return (y * g.astype(jnp.float32)).astype(x.dtype)


def _rope(x, pos, rot):
    half = rot // 2
    inv = ROPE_THETA ** (-jnp.arange(half, dtype=jnp.float32) / half)
    ang = pos.astype(jnp.float32)[:, None] * inv[None, :]
    shp = (ang.shape[0],) + (1,) * (x.ndim - 3) + (half,)
    cos = jnp.cos(ang).reshape(shp)
    sin = jnp.sin(ang).reshape(shp)
    xf = x.astype(jnp.float32)
    x1 = xf[..., :half]
    x2 = xf[..., half:rot]
    out = jnp.concatenate([x1 * cos - x2 * sin, x2 * cos + x1 * sin, xf[..., rot:]], axis=-1)
    return out.astype(x.dtype)


def _seg_softmax(scores, masks):
    parts = [s if m is None else jnp.where(m, s, NEG_INF) for s, m in zip(scores, masks)]
    sizes = [p.shape[-1] for p in parts]
    p = jax.nn.softmax(jnp.concatenate(parts, axis=-1), axis=-1)
    return jnp.split(p, [int(c) for c in np.cumsum(sizes)[:-1]], axis=-1)


def _diff_core(q, segs, lam):
    scale = DH_A ** -0.5
    scores = [jnp.einsum('bqkgcd,bskcd->bkgcqs', q, k).astype(jnp.float32) * scale for k, _, _ in segs]
    masks = [None if m is None else m[:, None, None, None] for _, _, m in segs]
    probs = _seg_softmax(scores, masks)
    outs = [jnp.einsum('bkgqs,bskv->bqkgv', (p[:, :, :, 0] - lam * p[:, :, :, 1]).astype(v.dtype), v)
            for p, (_, v, _) in zip(probs, segs)]
    return sum(outs[1:], outs[0])


def _mla_core(q_lat, q_rope, segs):
    scale = (NOPE_B + ROPE_B) ** -0.5
    scores = [(jnp.einsum('bqhc,bsc->bhqs', q_lat, c) + jnp.einsum('bqhr,bsr->bhqs', q_rope, r)).astype(jnp.float32) * scale
              for c, r, _ in segs]
    masks = [None if m is None else m[:, None] for _, _, m in segs]
    probs = _seg_softmax(scores, masks)
    outs = [jnp.einsum('bhqs,bsc->bqhc', p.astype(c.dtype), c) for p, (c, _, _) in zip(probs, segs)]
    return sum(outs[1:], outs[0])


def _sink_core(q, segs, sinks):
    scale = HD_C ** -0.5
    b, nq, _, g, _ = q.shape
    scores = [jnp.einsum('bqkgd,bskd->bkgqs', q, k).astype(jnp.float32) * scale for k, _, _ in segs]
    masks = [None if m is None else m[:, None, None] for _, _, m in segs]
    sink = jnp.broadcast_to(sinks.astype(jnp.float32)[None, :, :, None, None], (b, KV_C, g, nq, 1))
    probs = _seg_softmax(scores + [sink], masks + [None])[:-1]
    outs = [jnp.einsum('bkgqs,bskd->bqkgd', p.astype(v.dtype), v) for p, (_, v, _) in zip(probs, segs)]
    return sum(outs[1:], outs[0])


def _diff_lambda(lam_vecs, lam_init):
    l = lam_vecs.astype(jnp.float32)
    return jnp.exp(jnp.sum(l[0] * l[1])) - jnp.exp(jnp.sum(l[2] * l[3])) + lam_init


def _ab_project(h, pos, w_in, g_q, g_k, g_ql, w_qu, g_qb, g_kvl, g_kr, w_uk):
    b, s, _ = h.shape
    o1 = H_A * 2 * DH_A
    o2 = o1 + KV_A * 2 * DH_A
    o3 = o2 + KV_A * 2 * DH_A
    o4 = o3 + Q_LORA
    o5 = o4 + KV_LORA
    z = h @ w_in
    rot = DH_A // ROT_FRAC
    qa = _rope(_rms(z[..., :o1].reshape(b, s, KV_A, H_A // KV_A, 2, DH_A), g_q), pos, rot)
    ka = _rope(_rms(z[..., o1:o2].reshape(b, s, KV_A, 2, DH_A), g_k), pos, rot)
    va = z[..., o2:o3].reshape(b, s, KV_A, 2 * DH_A)
    cq = _rms(z[..., o3:o4], g_ql)
    qb = _rms((cq @ w_qu).reshape(b, s, H_B, NOPE_B + ROPE_B), g_qb)
    q_lat = jnp.einsum('bqhn,chn->bqhc', qb[..., :NOPE_B], w_uk)
    q_rope = _rope(qb[..., NOPE_B:], pos, ROPE_B)
    ckv = _rms(z[..., o4:o5], g_kvl)
    krope = _rope(_rms(z[..., o5:], g_kr), pos, ROPE_B)
    return qa, ka, va, q_lat, q_rope, ckv, krope


def _ab_prompt_attn(qa, ka, va, q_lat, q_rope, ckv, krope, lam):
    b, s = qa.shape[:2]
    nb = s // Q_BLOCK
    kpos = jnp.arange(s)

    def blk(args):
        j, qa_j, ql_j, qr_j = args
        qpos = j * Q_BLOCK + jnp.arange(Q_BLOCK)
        m = (kpos[None, :] <= qpos[:, None])[None]
        return _diff_core(qa_j, [(ka, va, m)], lam), _mla_core(ql_j, qr_j, [(ckv, krope, m)])

    to_blocks = lambda t: jnp.moveaxis(t.reshape((b, nb, Q_BLOCK) + t.shape[2:]), 1, 0)
    oa, ob = lax.map(blk, (jnp.arange(nb), to_blocks(qa), to_blocks(q_lat), to_blocks(q_rope)))
    from_blocks = lambda t: jnp.moveaxis(t, 0, 1).reshape((b, s) + t.shape[3:])
    return from_blocks(oa), from_blocks(ob)


def _ab_sample_attn(qa, ka, va, q_lat, q_rope, ckv, krope, lam, pk, pv, pc, pr):
    ds = qa.shape[1]
    m_new = jnp.tril(jnp.ones((ds, ds), dtype=bool))[None]
    oa = _diff_core(qa, [(pk, pv, None), (ka, va, m_new)], lam)
    ob = _mla_core(q_lat, q_rope, [(pc, pr, None), (ckv, krope, m_new)])
    return oa, ob


def _ab_output(oa, o_lat, lam_init, g_sub, w_uv, w_out):
    b, s = oa.shape[:2]
    oa = _rms(oa.reshape(b, s, H_A, 2 * DH_A), g_sub) * (1.0 - lam_init)
    ob = jnp.einsum('bqhc,chv->bqhv', o_lat, w_uv)
    merged = jnp.concatenate([oa.reshape(b, s, H_A * 2 * DH_A), ob.reshape(b, s, H_B * V_B)], axis=-1)
    return merged @ w_out


def _gather_pages(cache, i, page_table):
    g = cache[i, page_table]
    return g.reshape((g.shape[0], g.shape[1] * g.shape[2]) + g.shape[3:])


def _c_project(h, pos, w_in, g_q, g_k):
    b, s, _ = h.shape
    z = h @ w_in
    o1 = H_C * HD_C
    o2 = o1 + KV_C * HD_C
    rot = HD_C // ROT_FRAC
    q = _rope(_rms(z[..., :o1].reshape(b, s, KV_C, H_C // KV_C, HD_C), g_q), pos, rot)
    k = _rope(_rms(z[..., o1:o2].reshape(b, s, KV_C, HD_C), g_k), pos, rot)
    v = z[..., o2:].reshape(b, s, KV_C, HD_C)
    return q, k, v


def _c_prompt_attn(q, k, v, sinks):
    b, s = q.shape[:2]
    nb = s // WINDOW
    g = H_C // KV_C
    qb = q.reshape(b * nb, WINDOW, KV_C, g, HD_C)

    def band(t):
        tb = t.reshape((b, nb, WINDOW) + t.shape[2:])
        prev = jnp.pad(tb, ((0, 0), (1, 0)) + ((0, 0),) * (tb.ndim - 2))[:, :-1]
        return jnp.concatenate([prev, tb], axis=2).reshape((b * nb, 2 * WINDOW) + t.shape[2:])

    blk = jnp.arange(nb)
    qp = (blk[:, None] * WINDOW + jnp.arange(WINDOW)[None])[:, :, None]
    kp = ((blk[:, None] - 1) * WINDOW + jnp.arange(2 * WINDOW)[None])[:, None, :]
    m = (kp <= qp) & (kp >= qp - WINDOW) & (kp >= 0)
    m = jnp.broadcast_to(m[None], (b,) + m.shape).reshape(b * nb, WINDOW, 2 * WINDOW)
    o = _sink_core(qb, [(band(k), band(v), m)], sinks.reshape(KV_C, g))
    return o.reshape(b, s, H_C * HD_C)


def _c_sample_attn(q, k, v, sinks, buf_k, buf_v):
    b, ds = q.shape[:2]
    w = buf_k.shape[1]
    g = H_C // KV_C
    qpos = jnp.arange(ds)[:, None]
    bpos = (jnp.arange(w) - w)[None, :]
    npos = jnp.arange(ds)[None, :]
    m_buf = (bpos >= qpos - WINDOW)[None]
    m_new = ((npos <= qpos) & (npos >= qpos - WINDOW))[None]
    o = _sink_core(q, [(buf_k, buf_v, m_buf), (k, v, m_new)], sinks.reshape(KV_C, g))
    return o.reshape(b, ds, H_C * HD_C)


def _ffn(h, w_up, w_down):
    a = jax.nn.relu(h @ w_up)
    return (a * a) @ w_down


def setup_inputs(seed: int = 0) -> dict:
    key = jax.random.key(seed)
    keys = jax.random.split(key, 40)
    ks = iter([keys[j] for j in range(40)])
    nrm = lambda shape, scale: jax.random.normal(next(ks), shape, jnp.float32) * scale
    gain = lambda shape: 1.0 + 0.02 * jax.random.normal(next(ks), shape, jnp.float32)
    n_pages = PAST_LEN // PAGE_SIZE
    n_used = DEC_BATCH * n_pages
    n_pool = (n_used * 5) // 4
    page_table = jax.random.permutation(next(ks), n_pool)[:n_used].reshape(DEC_BATCH, n_pages).astype(jnp.int32)
    return {
        'x_prompt': nrm((BATCH, SEQ, D_MODEL), 1.0),
        'x_sample': nrm((DEC_BATCH, DEC_SEQ, D_MODEL), 1.0),
        'cache_diff_k': nrm((N_AB, n_pool, PAGE_SIZE, KV_A, 2, DH_A), 1.0),
        'cache_diff_v': nrm((N_AB, n_pool, PAGE_SIZE, KV_A, 2 * DH_A), 1.0),
        'cache_mla_ckv': nrm((N_AB, n_pool, PAGE_SIZE, KV_LORA), 1.0),
        'cache_mla_krope': nrm((N_AB, n_pool, PAGE_SIZE, ROPE_B), 1.0),
        'state_swa_k': nrm((N_C, DEC_BATCH, WINDOW, KV_C, HD_C), 1.0),
        'state_swa_v': nrm((N_C, DEC_BATCH, WINDOW, KV_C, HD_C), 1.0),
        'page_table': page_table,
        'g_mix': gain((DEPTH, D_MODEL)),
        'g_ffn': gain((DEPTH, D_MODEL)),
        'w_in_ab': nrm((N_AB, D_MODEL, IN_AB), D_MODEL ** -0.5),
        'g_q_a': gain((N_AB, DH_A)),
        'g_k_a': gain((N_AB, DH_A)),
        'lam_a': nrm((N_AB, 4, DH_A), 0.1),
        'g_sub_a': gain((N_AB, 2 * DH_A)),
        'g_qlat': gain((N_AB, Q_LORA)),
        'w_q_up': nrm((N_AB, Q_LORA, H_B * (NOPE_B + ROPE_B)), Q_LORA ** -0.5),
        'g_q_b': gain((N_AB, NOPE_B + ROPE_B)),
        'g_kvlat': gain((N_AB, KV_LORA)),
        'g_kr_b': gain((N_AB, ROPE_B)),
        'w_uk': nrm((N_AB, KV_LORA, H_B, NOPE_B), KV_LORA ** -0.5),
        'w_uv': nrm((N_AB, KV_LORA, H_B, V_B), KV_LORA ** -0.5),
        'w_out_ab': nrm((N_AB, MIX_AB, D_MODEL), MIX_AB ** -0.5),
        'w_in_c': nrm((N_C, D_MODEL, IN_C), D_MODEL ** -0.5),
        'g_q_c': gain((N_C, HD_C)),
        'g_k_c': gain((N_C, HD_C)),
        'sinks_c': nrm((N_C, H_C), 0.5),
        'w_out_c': nrm((N_C, H_C * HD_C, D_MODEL), (H_C * HD_C) ** -0.5),
        'w_up': nrm((DEPTH, D_MODEL, D_FF), D_MODEL ** -0.5),
        'w_down': nrm((DEPTH, D_FF, D_MODEL), D_FF ** -0.5),
    }


def reference(x_prompt, x_sample, cache_diff_k, cache_diff_v, cache_mla_ckv, cache_mla_krope,
              state_swa_k, state_swa_v, page_table, g_mix, g_ffn, w_in_ab, g_q_a, g_k_a, lam_a,
              g_sub_a, g_qlat, w_q_up, g_q_b, g_kvlat, g_kr_b, w_uk, w_uv, w_out_ab, w_in_c,
              g_q_c, g_k_c, sinks_c, w_out_c, w_up, w_down):
    s_p = x_prompt.shape[1]
    ds = x_sample.shape[1]
    past_len = page_table.shape[1] * cache_diff_k.shape[2]
    pos_p = jnp.arange(s_p)
    pos_s = past_len + jnp.arange(ds)
    yp, ys = x_prompt, x_sample
    dk_p, dv_p, ck_p, kr_p, sk_p, sv_p = [], [], [], [], [], []
    dk_s, dv_s, ck_s, kr_s, sk_s, sv_s = [], [], [], [], [], []
    for l in range(DEPTH):
        i = l // 2
        hp = _rms(yp, g_mix[l])
        hs = _rms(ys, g_mix[l])
        if l % 2 == 0:
            lam_init = 0.8 - 0.6 * math.exp(-0.3 * l)
            lam = _diff_lambda(lam_a[i], lam_init)
            proj = (w_in_ab[i], g_q_a[i], g_k_a[i], g_qlat[i], w_q_up[i], g_q_b[i], g_kvlat[i], g_kr_b[i], w_uk[i])
            outp = (lam_init, g_sub_a[i], w_uv[i], w_out_ab[i])
            qa, ka, va, ql, qr, ckv, kr = _ab_project(hp, pos_p, *proj)
            oa, ob = _ab_prompt_attn(qa, ka, va, ql, qr, ckv, kr, lam)
            yp = yp + _ab_output(oa, ob, *outp)
            dk_p.append(ka)
            dv_p.append(va)
            ck_p.append(ckv)
            kr_p.append(kr)
            qa, ka, va, ql, qr, ckv, kr = _ab_project(hs, pos_s, *proj)
            pk = _gather_pages(cache_diff_k, i, page_table)
            pv = _gather_pages(cache_diff_v, i, page_table)
            pc = _gather_pages(cache_mla_ckv, i, page_table)
            pr = _gather_pages(cache_mla_krope, i, page_table)
            oa, ob = _ab_sample_attn(qa, ka, va, ql, qr, ckv, kr, lam, pk, pv, pc, pr)
            ys = ys + _ab_output(oa, ob, *outp)
            dk_s.append(ka)
            dv_s.append(va)
            ck_s.append(ckv)
            kr_s.append(kr)
        else:
            q, k, v = _c_project(hp, pos_p, w_in_c[i], g_q_c[i], g_k_c[i])
            yp = yp + _c_prompt_attn(q, k, v, sinks_c[i]) @ w_out_c[i]
            sk_p.append(k[:, -WINDOW:])
            sv_p.append(v[:, -WINDOW:])
            q, k, v = _c_project(hs, pos_s, w_in_c[i], g_q_c[i], g_k_c[i])
            ys = ys + _c_sample_attn(q, k, v, sinks_c[i], state_swa_k[i], state_swa_v[i]) @ w_out_c[i]
            sk_s.append(jnp.concatenate([state_swa_k[i], k], axis=1)[:, -WINDOW:])
            sv_s.append(jnp.concatenate([state_swa_v[i], v], axis=1)[:, -WINDOW:])
        yp = yp + _ffn(_rms(yp, g_ffn[l]), w_up[l], w_down[l])
        ys = ys + _ffn(_rms(ys, g_ffn[l]), w_up[l], w_down[l])
    diff_k_prompt = jnp.stack(dk_p)
    diff_v_prompt = jnp.stack(dv_p)
    mla_ckv_prompt = jnp.stack(ck_p)
    mla_krope_prompt = jnp.stack(kr_p)
    swa_k_prompt = jnp.stack(sk_p)
    swa_v_prompt = jnp.stack(sv_p)
    diff_k_sample = jnp.stack(dk_s)
    diff_v_sample = jnp.stack(dv_s)
    mla_ckv_sample = jnp.stack(ck_s)
    mla_krope_sample = jnp.stack(kr_s)
    swa_k_sample = jnp.stack(sk_s)
    swa_v_sample = jnp.stack(sv_s)
    return (yp, ys, diff_k_prompt, diff_v_prompt, mla_ckv_prompt, mla_krope_prompt, swa_k_prompt, swa_v_prompt,
            diff_k_sample, diff_v_sample, mla_ckv_sample, mla_krope_sample, swa_k_sample, swa_v_sample)
```

```python
import functools
import math

import jax
import jax.numpy as jnp
from jax import lax
from jax.experimental import pallas as pl
from jax.experimental.pallas import tpu as pltpu

H_A, KV_A, DH_A = 8, 2, 64
G_A = H_A // KV_A
H_B, Q_LORA, KV_LORA, NOPE_B, ROPE_B, V_B = 8, 512, 256, 128, 64, 128
H_C, KV_C, HD_C = 32, 4, 64
G_C = H_C // KV_C
WINDOW = 128
ROPE_THETA = 500000.0
ROT_FRAC = 4
EPS = 1e-6
NEG_INF = -1e30

QA_W = H_A * 2 * DH_A
KA_W = KV_A * 2 * DH_A
VA_W = KV_A * 2 * DH_A
QCAT_W = KV_LORA + 2 * ROPE_B
IN_AB_PAD = QA_W + KA_W + VA_W + Q_LORA + KV_LORA + 2 * ROPE_B
QC_W = H_C * HD_C
KC_W = KV_C * HD_C

LANES = 128
VMEM_LIMIT = 56 * 1024 * 1024

F32 = jnp.float32
BF16 = jnp.bfloat16


def _cparams(sem):
    return pltpu.CompilerParams(dimension_semantics=sem, vmem_limit_bytes=VMEM_LIMIT)


def _const_spec(shape):
    nd = len(shape)
    return pl.BlockSpec(shape, lambda *_: (0,) * nd, pipeline_mode=pl.Buffered(1))


def _dot(a, b):
    return jnp.dot(a, b, preferred_element_type=F32)


def _dot_nt(a, b):
    return lax.dot_general(a, b, (((1,), (1,)), ((), ())), preferred_element_type=F32)


def _rms(x, g):
    ms = jnp.mean(x * x, axis=-1, keepdims=True)
    return x * lax.rsqrt(ms + EPS) * g


def _lo_mask(rows):
    return lax.broadcasted_iota(jnp.int32, (rows, LANES), 1) < (LANES // 2)


def _norm64(chunk, gain, lo):
    sq = chunk * chunk
    ss_lo = jnp.sum(jnp.where(lo, sq, 0.0), axis=-1, keepdims=True)
    ss_hi = jnp.sum(jnp.where(lo, 0.0, sq), axis=-1, keepdims=True)
    ms = jnp.where(lo, ss_lo, ss_hi) * (1.0 / 64.0)
    return chunk * lax.rsqrt(ms + EPS) * gain


def _rope(y, tab_ref, half):
    up = pltpu.roll(y, LANES - half, 1)
    down = pltpu.roll(y, half, 1)
    return y * tab_ref[0] + up * tab_ref[1] + down * tab_ref[2]


def _ab_proj_kernel(x_ref, gmix_ref, win_ref, wqup_ref, wuk_ref, gq_ref, gk_ref, gql_ref,
                    gqbn_ref, gqbr_ref, gkvl_ref, gkr_ref, taba_ref, tabb_ref,
                    q_ref, k32_ref, v32_ref, ckv32_ref, kr32_ref, kbf_ref, vbf_ref, kcat_ref,
                    qcat_ref):
    tm = x_ref.shape[0]
    lo = _lo_mask(tm)
    h = _rms(x_ref[...], gmix_ref[...]).astype(BF16)
    z = _dot(h, win_ref[...])
    scale_a = DH_A ** -0.5
    scale_b = (NOPE_B + ROPE_B) ** -0.5
    rot_a = DH_A // ROT_FRAC

    for j in range(QA_W // LANES):
        c = z[:, j * LANES:(j + 1) * LANES]
        r = _rope(_norm64(c, gq_ref[...], lo), taba_ref, rot_a // 2)
        q_ref[:, j * LANES:(j + 1) * LANES] = (r * scale_a).astype(BF16)
    for j in range(KA_W // LANES):
        c = z[:, QA_W + j * LANES:QA_W + (j + 1) * LANES]
        r = _rope(_norm64(c, gk_ref[...], lo), taba_ref, rot_a // 2)
        k32_ref[:, j * LANES:(j + 1) * LANES] = r
        kbf_ref[:, j * LANES:(j + 1) * LANES] = r.astype(BF16)
    o = QA_W + KA_W
    v = z[:, o:o + VA_W]
    v32_ref[...] = v
    vbf_ref[...] = v.astype(BF16)
    o += VA_W

    cq = _rms(z[:, o:o + Q_LORA], gql_ref[...]).astype(BF16)
    o += Q_LORA
    qb = _dot(cq, wqup_ref[...])
    nope_w = H_B * NOPE_B
    ss_n = []
    for hh in range(H_B):
        c = qb[:, hh * NOPE_B:(hh + 1) * NOPE_B]
        ss_n.append(jnp.sum(c * c, axis=-1, keepdims=True))
    inv_hd = 1.0 / (NOPE_B + ROPE_B)
    for i in range(H_B // 2):
        rc = qb[:, nope_w + i * LANES:nope_w + (i + 1) * LANES]
        sq = rc * rc
        ss_lo = jnp.sum(jnp.where(lo, sq, 0.0), axis=-1, keepdims=True)
        ss_hi = jnp.sum(jnp.where(lo, 0.0, sq), axis=-1, keepdims=True)
        r_e = lax.rsqrt((ss_n[2 * i] + ss_lo) * inv_hd + EPS)
        r_o = lax.rsqrt((ss_n[2 * i + 1] + ss_hi) * inv_hd + EPS)
        yr = rc * jnp.where(lo, r_e, r_o) * gqbr_ref[...]
        qr = _rope(yr, tabb_ref, ROPE_B // 2) * scale_b
        for hh, r_h, keep_lo in ((2 * i, r_e, True), (2 * i + 1, r_o, False)):
            qn = (qb[:, hh * NOPE_B:(hh + 1) * NOPE_B] * r_h * gqbn_ref[...]).astype(BF16)
            ql = _dot(qn, wuk_ref[hh]) * scale_b
            qcat_ref[:, hh * QCAT_W:hh * QCAT_W + KV_LORA] = ql.astype(BF16)
            tail = jnp.where(lo, qr, 0.0) if keep_lo else jnp.where(lo, 0.0, qr)
            qcat_ref[:, hh * QCAT_W + KV_LORA:(hh + 1) * QCAT_W] = tail.astype(BF16)

    ckv = _rms(z[:, o:o + KV_LORA], gkvl_ref[...])
    o += KV_LORA
    ckv32_ref[...] = ckv
    kcat_ref[:, 0:KV_LORA] = ckv.astype(BF16)
    kr = _rope(_norm64(z[:, o:o + LANES], gkr_ref[...], lo), tabb_ref, ROPE_B // 2)
    kr32_ref[...] = kr[:, 0:ROPE_B]
    kcat_ref[:, KV_LORA:KV_LORA + LANES] = kr.astype(BF16)


def _ab_project(x, pos_tiles, gmix, w, taba, tabb):
    t, d = x.shape
    tm = min(256, t)
    n_pos = pos_tiles // tm if pos_tiles >= tm else 1
    tab_spec = pl.BlockSpec((3, tm, LANES), lambda i: (0, i % n_pos, 0))
    row = lambda wdt: pl.BlockSpec((tm, wdt), lambda i: (i, 0))
    out_shapes = (
        jax.ShapeDtypeStruct((t, QA_W), BF16),
        jax.ShapeDtypeStruct((t, KA_W), F32),
        jax.ShapeDtypeStruct((t, VA_W), F32),
        jax.ShapeDtypeStruct((t, KV_LORA), F32),
        jax.ShapeDtypeStruct((t, ROPE_B), F32),
        jax.ShapeDtypeStruct((t, KA_W), BF16),
        jax.ShapeDtypeStruct((t, VA_W), BF16),
        jax.ShapeDtypeStruct((t, QCAT_W), BF16),
        jax.ShapeDtypeStruct((t, H_B * QCAT_W), BF16),
    )
    out_specs = (row(QA_W), row(KA_W), row(VA_W), row(KV_LORA), row(ROPE_B), row(KA_W),
                 row(VA_W), row(QCAT_W), row(H_B * QCAT_W))
    in_specs = [
        row(d), _const_spec((1, d)), _const_spec(w["w_in"].shape), _const_spec(w["w_qup"].shape),
        _const_spec(w["w_uk"].shape), _const_spec((1, LANES)), _const_spec((1, LANES)),
        _const_spec((1, Q_LORA)), _const_spec((1, NOPE_B)), _const_spec((1, LANES)),
        _const_spec((1, KV_LORA)), _const_spec((1, LANES)), tab_spec, tab_spec,
    ]
    return pl.pallas_call(
        _ab_proj_kernel, grid=(t // tm,), in_specs=in_specs, out_specs=out_specs,
        out_shape=out_shapes, compiler_params=_cparams(("parallel",)), name="ab_proj",
    )(x, gmix, w["w_in"], w["w_qup"], w["w_uk"], w["g_q"], w["g_k"], w["g_ql"], w["g_qbn"],
      w["g_qbr"], w["g_kvl"], w["g_kr"], taba, tabb)


def _c_proj_kernel(x_ref, gmix_ref, win_ref, gq_ref, gk_ref, taba_ref,
                   q_ref, k32_ref, v32_ref, kbf_ref, vbf_ref):
    tm = x_ref.shape[0]
    lo = _lo_mask(tm)
    h = _rms(x_ref[...], gmix_ref[...]).astype(BF16)
    z = _dot(h, win_ref[...])
    scale = HD_C ** -0.5
    half = HD_C // ROT_FRAC // 2
    for j in range(QC_W // LANES):
        c = z[:, j * LANES:(j + 1) * LANES]
        r = _rope(_norm64(c, gq_ref[...], lo), taba_ref, half)
        q_ref[:, j * LANES:(j + 1) * LANES] = (r * scale).astype(BF16)
    for j in range(KC_W // LANES):
        c = z[:, QC_W + j * LANES:QC_W + (j + 1) * LANES]
        r = _rope(_norm64(c, gk_ref[...], lo), taba_ref, half)
        k32_ref[:, j * LANES:(j + 1) * LANES] = r
        kbf_ref[:, j * LANES:(j + 1) * LANES] = r.astype(BF16)
    v = z[:, QC_W + KC_W:QC_W + 2 * KC_W]
    v32_ref[...] = v
    vbf_ref[...] = v.astype(BF16)


def _c_project(x, pos_tiles, gmix, w, taba):
    t, d = x.shape
    tm = min(256, t)
    n_pos = pos_tiles // tm if pos_tiles >= tm else 1
    tab_spec = pl.BlockSpec((3, tm, LANES), lambda i: (0, i % n_pos, 0))
    row = lambda wdt: pl.BlockSpec((tm, wdt), lambda i: (i, 0))
    out_shapes = (
        jax.ShapeDtypeStruct((t, QC_W), BF16),
        jax.ShapeDtypeStruct((t, KC_W), F32),
        jax.ShapeDtypeStruct((t, KC_W), F32),
        jax.ShapeDtypeStruct((t, KC_W), BF16),
        jax.ShapeDtypeStruct((t, KC_W), BF16),
    )
    in_specs = [row(d), _const_spec((1, d)), _const_spec(w["w_in"].shape),
                _const_spec((1, LANES)), _const_spec((1, LANES)), tab_spec]
    return pl.pallas_call(
        _c_proj_kernel, grid=(t // tm,), in_specs=in_specs,
        out_specs=(row(QC_W), row(KC_W), row(KC_W), row(KC_W), row(KC_W)),
        out_shape=out_shapes, compiler_params=_cparams(("parallel",)), name="c_proj",
    )(x, gmix, w["w_in"], w["g_q"], w["g_k"], taba)


def _ab_out_kernel(oa_ref, ol_ref, wuv_ref, wo_ref, x_ref, o_ref):
    half = H_A * 2 * DH_A
    acc = x_ref[...] + _dot(oa_ref[...], wo_ref[0:half, :])
    obs = [_dot(ol_ref[:, hh * KV_LORA:(hh + 1) * KV_LORA], wuv_ref[hh]).astype(BF16)
           for hh in range(H_B)]
    ob = jnp.concatenate(obs, axis=1)
    o_ref[...] = acc + _dot(ob, wo_ref[half:, :])


def _ab_out(oa, ol, w, x):
    t, d = x.shape
    tm = min(256, t)
    row = lambda wdt: pl.BlockSpec((tm, wdt), lambda i: (i, 0))
    return pl.pallas_call(
        _ab_out_kernel, grid=(t // tm,),
        in_specs=[row(oa.shape[1]), row(ol.shape[1]), _const_spec(w["w_uv"].shape),
                  _const_spec(w["w_out"].shape), row(d)],
        out_specs=row(d), out_shape=jax.ShapeDtypeStruct((t, d), F32),
        compiler_params=_cparams(("parallel",)), name="ab_out",
    )(oa, ol, w["w_uv"], w["w_out"], x)


def _c_out_kernel(a_ref, wo_ref, x_ref, o_ref):
    o_ref[...] = x_ref[...] + _dot(a_ref[...], wo_ref[...])


def _c_out(a, w_out, x):
    t, d = x.shape
    tm = min(256, t)
    row = lambda wdt: pl.BlockSpec((tm, wdt), lambda i: (i, 0))
    return pl.pallas_call(
        _c_out_kernel, grid=(t // tm,),
        in_specs=[row(a.shape[1]), _const_spec(w_out.shape), row(d)],
        out_specs=row(d), out_shape=jax.ShapeDtypeStruct((t, d), F32),
        compiler_params=_cparams(("parallel",)), name="c_out",
    )(a, w_out, x)


def _ffn_kernel(x_ref, g_ref, wup_ref, wdn_ref, o_ref, h_scr, acc_scr):
    j = pl.program_id(1)

    @pl.when(j == 0)
    def _():
        x = x_ref[...]
        h_scr[...] = _rms(x, g_ref[...]).astype(BF16)
        acc_scr[...] = x

    a = jnp.maximum(_dot(h_scr[...], wup_ref[...]), 0.0)
    acc_scr[...] += _dot((a * a).astype(BF16), wdn_ref[...])

    @pl.when(j == pl.num_programs(1) - 1)
    def _():
        o_ref[...] = acc_scr[...]


def _ffn(x, g, w_up, w_dn):
    t, d = x.shape
    f = w_up.shape[1]
    tm = min(512, t)
    tf = min(512, f)
    return pl.pallas_call(
        _ffn_kernel, grid=(t // tm, f // tf),
        in_specs=[pl.BlockSpec((tm, d), lambda i, j: (i, 0)), _const_spec((1, d)),
                  pl.BlockSpec((d, tf), lambda i, j: (0, j)),
                  pl.BlockSpec((tf, d), lambda i, j: (j, 0))],
        out_specs=pl.BlockSpec((tm, d), lambda i, j: (i, 0)),
        out_shape=jax.ShapeDtypeStruct((t, d), F32),
        scratch_shapes=[pltpu.VMEM((tm, d), BF16), pltpu.VMEM((tm, d), F32)],
        compiler_params=_cparams(("parallel", "arbitrary")), name="ffn",
    )(x, g, w_up, w_dn)


def _diff_lambda(lam_ref, lam_init):
    l = lam_ref[...]
    a = jnp.sum(l[0:1] * l[1:2], axis=-1, keepdims=True)
    b = jnp.sum(l[2:3] * l[3:4], axis=-1, keepdims=True)
    return jnp.exp(a) - jnp.exp(b) + lam_init


def _flash_update(s, v, m_scr, l_scr, acc_scr, idx):
    m_prev = m_scr[idx]
    m_new = jnp.maximum(m_prev, jnp.max(s, axis=-1, keepdims=True))
    alpha = jnp.exp(m_prev - m_new)
    p = jnp.exp(s - m_new[:, 0:1])
    l_scr[idx] = alpha * l_scr[idx] + jnp.sum(p, axis=-1, keepdims=True)
    acc_scr[idx] = alpha[:, 0:1] * acc_scr[idx] + _dot(p.astype(BF16), v)
    m_scr[idx] = m_new


def _diff_attn_kernel(lam_ref, gsub_ref, q_ref, k_ref, v_ref, o_ref, m_scr, l_scr, acc_scr,
                      *, lam_init, tq):
    qi = pl.program_id(2)
    rows = G_A * tq
    q4 = jnp.concatenate([q_ref[:, g * LANES:(g + 1) * LANES] for g in range(G_A)], axis=0)
    m_scr[...] = jnp.full(m_scr.shape, NEG_INF, F32)
    l_scr[...] = jnp.zeros(l_scr.shape, F32)
    acc_scr[...] = jnp.zeros(acc_scr.shape, F32)
    lo_k = _lo_mask(tq)

    def tile(ki, masked):
        ks = pl.multiple_of(ki * tq, tq)
        kt = k_ref[pl.ds(ks, tq), :]
        vt = v_ref[pl.ds(ks, tq), :]
        zero = jnp.zeros_like(kt)
        for c, kc in enumerate((jnp.where(lo_k, kt, zero), jnp.where(lo_k, zero, kt))):
            s = _dot_nt(q4, kc)
            if masked:
                r = lax.broadcasted_iota(jnp.int32, (rows, tq), 0) % tq
                cc = lax.broadcasted_iota(jnp.int32, (rows, tq), 1)
                s = jnp.where(cc <= r, s, NEG_INF)
            _flash_update(s, vt, m_scr, l_scr, acc_scr, c)

    def body(ki, carry):
        tile(ki, False)
        return carry

    lax.fori_loop(0, qi, body, 0)
    tile(qi, True)

    lam = _diff_lambda(lam_ref, lam_init)
    out = acc_scr[0] / l_scr[0] - lam * (acc_scr[1] / l_scr[1])
    out = _rms(out, gsub_ref[...]) * (1.0 - lam_init)
    for g in range(G_A):
        o_ref[:, g * LANES:(g + 1) * LANES] = out[g * tq:(g + 1) * tq].astype(BF16)


def _diff_attn(q, k, v, lam_vecs, g_sub, lam_init, batch, seq):
    tq = min(256, seq)
    nq = seq // tq
    gw = G_A * LANES
    kern = functools.partial(_diff_attn_kernel, lam_init=lam_init, tq=tq)
    return pl.pallas_call(
        kern, grid=(batch, KV_A, nq),
        in_specs=[_const_spec(lam_vecs.shape), _const_spec(g_sub.shape),
                  pl.BlockSpec((tq, gw), lambda b, kv, i: (b * nq + i, kv)),
                  pl.BlockSpec((seq, LANES), lambda b, kv, i: (b, kv)),
                  pl.BlockSpec((seq, LANES), lambda b, kv, i: (b, kv))],
        out_specs=pl.BlockSpec((tq, gw), lambda b, kv, i: (b * nq + i, kv)),
        out_shape=jax.ShapeDtypeStruct(q.shape, BF16),
        scratch_shapes=[pltpu.VMEM((2, G_A * tq, LANES), F32)] * 3,
        compiler_params=_cparams(("parallel", "parallel", "arbitrary")), name="diff_attn",
    )(lam_vecs, g_sub, q, k, v)


def _mla_attn_kernel(q_ref, k_ref, o_ref, m_scr, l_scr, acc_scr, *, tq):
    qi = pl.program_id(1)
    rows = H_B * tq
    q8 = jnp.concatenate([q_ref[:, hh * QCAT_W:(hh + 1) * QCAT_W] for hh in range(H_B)], axis=0)
    m_scr[...] = jnp.full(m_scr.shape, NEG_INF, F32)
    l_scr[...] = jnp.zeros(l_scr.shape, F32)
    acc_scr[...] = jnp.zeros(acc_scr.shape, F32)

    def tile(ki, masked):
        ks = pl.multiple_of(ki * tq, tq)
        kt = k_ref[pl.ds(ks, tq), :]
        s = _dot_nt(q8, kt)
        if masked:
            r = lax.broadcasted_iota(jnp.int32, (rows, tq), 0) % tq
            cc = lax.broadcasted_iota(jnp.int32, (rows, tq), 1)
            s = jnp.where(cc <= r, s, NEG_INF)
        _flash_update(s, kt[:, 0:KV_LORA], m_scr, l_scr, acc_scr, 0)

    def body(ki, carry):
        tile(ki, False)
        return carry

    lax.fori_loop(0, qi, body, 0)
    tile(qi, True)
    out = acc_scr[0] / l_scr[0][:, 0:1]
    for hh in range(H_B):
        o_ref[:, hh * KV_LORA:(hh + 1) * KV_LORA] = out[hh * tq:(hh + 1) * tq].astype(BF16)


def _mla_attn(qcat, kcat, batch, seq):
    tq = min(256, seq)
    nq = seq // tq
    kern = functools.partial(_mla_attn_kernel, tq=tq)
    return pl.pallas_call(
        kern, grid=(batch, nq),
        in_specs=[pl.BlockSpec((tq, H_B * QCAT_W), lambda b, i: (b * nq + i, 0)),
                  pl.BlockSpec((seq, QCAT_W), lambda b, i: (b, 0))],
        out_specs=pl.BlockSpec((tq, H_B * KV_LORA), lambda b, i: (b * nq + i, 0)),
        out_shape=jax.ShapeDtypeStruct((batch * seq, H_B * KV_LORA), BF16),
        scratch_shapes=[pltpu.VMEM((1, H_B * tq, LANES), F32), pltpu.VMEM((1, H_B * tq, LANES), F32),
                        pltpu.VMEM((1, H_B * tq, KV_LORA), F32)],
        compiler_params=_cparams(("parallel", "arbitrary")), name="mla_attn",
    )(qcat, kcat)


def _swa_attn_kernel(sink_ref, q_ref, kc_ref, kp_ref, vc_ref, vp_ref, o_ref):
    j = pl.program_id(1)
    w = WINDOW
    keys = jnp.concatenate([kp_ref[...], kc_ref[...]], axis=0)
    vals = jnp.concatenate([vp_ref[...], vc_ref[...]], axis=0)
    rows = G_C * w
    r = lax.broadcasted_iota(jnp.int32, (rows, 2 * w), 0) % w
    c = lax.broadcasted_iota(jnp.int32, (rows, 2 * w), 1)
    allowed = (c >= r) & (c <= r + w) & ((j > 0) | (c >= w))
    lo = _lo_mask(2 * w)
    for kvp in range(KV_C // 2):
        q8 = jnp.concatenate(
            [q_ref[:, (kvp * G_C + g) * LANES:(kvp * G_C + g + 1) * LANES] for g in range(G_C)],
            axis=0)
        kk = keys[:, kvp * LANES:(kvp + 1) * LANES]
        vv = vals[:, kvp * LANES:(kvp + 1) * LANES]
        zero = jnp.zeros_like(kk)
        outs = []
        for par in range(2):
            ksel = jnp.where(lo, kk, zero) if par == 0 else jnp.where(lo, zero, kk)
            vsel = jnp.where(lo, vv, zero) if par == 0 else jnp.where(lo, zero, vv)
            kv = 2 * kvp + par
            sink = jnp.concatenate(
                [jnp.full((w, 1), sink_ref[kv * G_C + g], F32) for g in range(G_C)], axis=0)
            s = jnp.where(allowed, _dot_nt(q8, ksel), NEG_INF)
            m = jnp.maximum(jnp.max(s, axis=-1, keepdims=True), sink)
            e = jnp.exp(s - m)
            denom = jnp.sum(e, axis=-1, keepdims=True) + jnp.exp(sink - m)
            outs.append(_dot((e / denom).astype(BF16), vsel))
        out = outs[0] + outs[1]
        for g in range(G_C):
            o_ref[:, (kvp * G_C + g) * LANES:(kvp * G_C + g + 1) * LANES] = (
                out[g * w:(g + 1) * w].astype(BF16))


def _swa_attn(q, k, v, sinks, batch, seq):
    nb = seq // WINDOW
    cur = lambda b, j: (b * nb + j, 0)
    prev = lambda b, j: (b * nb + jnp.maximum(j - 1, 0), 0)
    return pl.pallas_call(
        _swa_attn_kernel, grid=(batch, nb),
        in_specs=[pl.BlockSpec(memory_space=pltpu.SMEM),
                  pl.BlockSpec((WINDOW, QC_W), cur),
                  pl.BlockSpec((WINDOW, KC_W), cur), pl.BlockSpec((WINDOW, KC_W), prev),
                  pl.BlockSpec((WINDOW, KC_W), cur), pl.BlockSpec((WINDOW, KC_W), prev)],
        out_specs=pl.BlockSpec((WINDOW, QC_W), cur),
        out_shape=jax.ShapeDtypeStruct(q.shape, BF16),
        compiler_params=_cparams(("parallel", "parallel")), name="swa_attn",
    )(sinks, q, k, k, v, v)


PAGES_PER_CHUNK = 8


def _dec_ab_kernel(pt_ref, lam_ref, gsub_ref, qz_ref, knew_ref, vnew_ref, qc_ref, cnew_ref,
                   rnew_ref, kc_hbm, vc_hbm, cc_hbm, rc_hbm, oa_ref, ol_ref,
                   kbuf, vbuf, cbuf, rbuf, sem, *, layer, lam_init, n_pages):
    b = pl.program_id(0)
    pp = PAGES_PER_CHUNK
    n_chunks = n_pages // pp
    page_size = vbuf.shape[2] // KV_A
    streams = ((kc_hbm, kbuf), (vc_hbm, vbuf), (cc_hbm, cbuf), (rc_hbm, rbuf))

    def copies(chunk, slot):
        out = []
        for j in range(pp):
            page = pt_ref[b, chunk * pp + j]
            for si, (src, dst) in enumerate(streams):
                out.append(pltpu.make_async_copy(src.at[layer, page], dst.at[slot, j],
                                                 sem.at[si, slot]))
        return out

    def start(chunk, slot):
        for cp in copies(chunk, slot):
            cp.start()

    def wait(chunk, slot):
        for cp in copies(chunk, slot):
            cp.wait()

    qz = qz_ref[0].astype(BF16)
    qc = qc_ref[0]
    ql = qc[:, 0:KV_LORA].astype(BF16)
    qt = qc[:, KV_LORA:QCAT_W].astype(BF16)
    row_kv0 = (lax.broadcasted_iota(jnp.int32, (2 * H_A, LANES), 0) // G_A) % KV_A == 0

    knew = knew_ref[0]
    vnew = vnew_ref[0]
    m_a = jnp.sum(qz_ref[0] * knew, axis=-1, keepdims=True)
    l_a = jnp.ones_like(m_a)
    acc_a = jnp.where(row_kv0, vnew[:, 0:LANES], vnew[:, LANES:2 * LANES])
    cnew = cnew_ref[0]
    rnew = rnew_ref[0]
    m_b = (jnp.sum(qc[:, 0:KV_LORA] * cnew, axis=-1, keepdims=True)
           + jnp.sum(qc[:, KV_LORA:QCAT_W] * rnew, axis=-1, keepdims=True))
    l_b = jnp.ones_like(m_b)
    acc_b = jnp.broadcast_to(cnew, (H_B, KV_LORA))

    start(0, 0)

    def body(ci, carry):
        m_a, l_a, acc_a, m_b, l_b, acc_b = carry
        slot = ci % 2

        @pl.when(ci + 1 < n_chunks)
        def _():
            start(ci + 1, 1 - slot)

        wait(ci, slot)

        s = jnp.concatenate([_dot(qz, kbuf[slot, j].astype(BF16)) for j in range(pp)], axis=1)
        m_new = jnp.maximum(m_a, jnp.max(s, axis=-1, keepdims=True))
        alpha = jnp.exp(m_a - m_new)
        p = jnp.exp(s - m_new)
        l_a = alpha * l_a + jnp.sum(p, axis=-1, keepdims=True)
        pv = jnp.zeros_like(acc_a)
        for j in range(pp):
            pj = p[:, j * page_size:(j + 1) * page_size].astype(BF16)
            v0 = vbuf[slot, j, pl.ds(0, page_size, stride=KV_A), :].astype(BF16)
            v1 = vbuf[slot, j, pl.ds(1, page_size, stride=KV_A), :].astype(BF16)
            pv = pv + jnp.where(row_kv0, _dot(pj, v0), _dot(pj, v1))
        acc_a = alpha * acc_a + pv
        m_a = m_new

        cks = [cbuf[slot, j].astype(BF16) for j in range(pp)]
        sb = []
        for j in range(pp):
            rt = rbuf[slot, j].astype(BF16)
            sb.append(_dot_nt(ql, cks[j]) + _dot(qt, jnp.concatenate([rt, rt], axis=0)))
        s = jnp.concatenate(sb, axis=1)
        m_new = jnp.maximum(m_b, jnp.max(s, axis=-1, keepdims=True))
        alpha = jnp.exp(m_b - m_new)
        p = jnp.exp(s - m_new)
        l_b = alpha * l_b + jnp.sum(p, axis=-1, keepdims=True)
        pv = jnp.zeros_like(acc_b)
        for j in range(pp):
            pv = pv + _dot(p[:, j * page_size:(j + 1) * page_size].astype(BF16), cks[j])
        acc_b = alpha * acc_b + pv
        return m_a, l_a, acc_a, m_new, l_b, acc_b

    m_a, l_a, acc_a, m_b, l_b, acc_b = lax.fori_loop(
        0, n_chunks, body, (m_a, l_a, acc_a, m_b, l_b, acc_b))

    out = acc_a / l_a
    lam = _diff_lambda(lam_ref, lam_init)
    oa = out[0:H_A] - lam * out[H_A:2 * H_A]
    oa_ref[0] = _rms(oa, gsub_ref[...]) * (1.0 - lam_init)
    ol_ref[0] = acc_b / l_b


def _dec_ab_attn(page_table, lam_vecs, g_sub, qz, knew, vnew, qc, cnew, rnew,
                 cache_k, cache_v, cache_c, cache_r, layer, lam_init):
    db, n_pages = page_table.shape
    pp = PAGES_PER_CHUNK
    kern = functools.partial(_dec_ab_kernel, layer=layer, lam_init=lam_init, n_pages=n_pages)
    seq_spec = lambda shape: pl.BlockSpec((1,) + shape, lambda b, pt: (b, 0, 0))
    anyspec = pl.BlockSpec(memory_space=pl.ANY)
    grid_spec = pltpu.PrefetchScalarGridSpec(
        num_scalar_prefetch=1, grid=(db,),
        in_specs=[pl.BlockSpec(lam_vecs.shape, lambda b, pt: (0, 0), pipeline_mode=pl.Buffered(1)),
                  pl.BlockSpec(g_sub.shape, lambda b, pt: (0, 0), pipeline_mode=pl.Buffered(1)),
                  seq_spec(qz.shape[1:]), seq_spec(knew.shape[1:]), seq_spec(vnew.shape[1:]),
                  seq_spec(qc.shape[1:]), seq_spec(cnew.shape[1:]), seq_spec(rnew.shape[1:]),
                  anyspec, anyspec, anyspec, anyspec],
        out_specs=[seq_spec((H_A, LANES)), seq_spec((H_B, KV_LORA))],
        scratch_shapes=[pltpu.VMEM((2, pp) + cache_k.shape[2:], F32),
                        pltpu.VMEM((2, pp) + cache_v.shape[2:], F32),
                        pltpu.VMEM((2, pp) + cache_c.shape[2:], F32),
                        pltpu.VMEM((2, pp) + cache_r.shape[2:], F32),
                        pltpu.SemaphoreType.DMA((4, 2))])
    return pl.pallas_call(
        kern, grid_spec=grid_spec,
        out_shape=(jax.ShapeDtypeStruct((db, H_A, LANES), F32),
                   jax.ShapeDtypeStruct((db, H_B, KV_LORA), F32)),
        compiler_params=_cparams(("arbitrary",)), name="dec_ab_attn",
    )(page_table, lam_vecs, g_sub, qz, knew, vnew, qc, cnew, rnew, cache_k, cache_v, cache_c,
      cache_r)


def _dec_c_kernel(sink_ref, qz_ref, kt_ref, vt_ref, knew_ref, vnew_ref, o_ref):
    qf = qz_ref[0]
    kt = kt_ref[0, 0].astype(BF16)
    vt = vt_ref[0, 0].astype(BF16)
    sink = sink_ref[...]
    s = _dot(qf.astype(BF16), kt)
    s_new = jnp.sum(qf * knew_ref[0], axis=-1, keepdims=True)
    m = jnp.maximum(jnp.maximum(jnp.max(s, axis=-1, keepdims=True), s_new), sink)
    e = jnp.exp(s - m)
    e_new = jnp.exp(s_new - m)
    denom = jnp.sum(e, axis=-1, keepdims=True) + e_new + jnp.exp(sink - m)
    o = (_dot_nt(e.astype(BF16), vt) + e_new * vnew_ref[0]) / denom
    rkv = lax.broadcasted_iota(jnp.int32, o.shape, 0) // G_C
    ckv = lax.broadcasted_iota(jnp.int32, o.shape, 1) // HD_C
    o_ref[0] = jnp.where(rkv == ckv, o, 0.0)


def _dec_c_attn(sink_col, qz, state_kt, state_vt, knew, vnew, layer):
    db = qz.shape[0]
    w = state_kt.shape[-1]
    seq_spec = lambda shape: pl.BlockSpec((1,) + shape, lambda b: (b, 0, 0))
    st_spec = pl.BlockSpec((1, 1, KC_W, w), lambda b: (layer, b, 0, 0))
    return pl.pallas_call(
        _dec_c_kernel, grid=(db,),
        in_specs=[_const_spec(sink_col.shape), seq_spec(qz.shape[1:]), st_spec, st_spec,
                  seq_spec(knew.shape[1:]), seq_spec(vnew.shape[1:])],
        out_specs=seq_spec((H_C, KC_W)),
        out_shape=jax.ShapeDtypeStruct((db, H_C, KC_W), F32),
        compiler_params=_cparams(("parallel",)), name="dec_c_attn",
    )(sink_col, qz, state_kt, state_vt, knew, vnew)


def _rope_tables(pos, rot):
    half = rot // 2
    inv = ROPE_THETA ** (-jnp.arange(half, dtype=F32) / half)
    ang = pos.astype(F32)[:, None] * inv[None, :]
    cos, sin = jnp.cos(ang), jnp.sin(ang)
    jj = jnp.arange(LANES) % 64
    idx = jj % half
    c = jnp.where(jj < rot, cos[:, idx], 1.0)
    s1 = jnp.where(jj < half, -sin[:, idx], 0.0)
    s2 = jnp.where((jj >= half) & (jj < rot), sin[:, idx], 0.0)
    return jnp.stack([c, s1, s2]).astype(F32)


def _tile2(g):
    return jnp.concatenate([g, g]).reshape(1, -1).astype(F32)


def _prep_ab(i, w_in_ab, g_q_a, g_k_a, g_qlat, w_q_up, g_q_b, g_kvlat, g_kr_b, w_uk, w_uv,
             w_out_ab):
    w_in = w_in_ab[i]
    kr_cols = w_in[:, -ROPE_B:]
    w_in = jnp.concatenate([w_in, kr_cols], axis=1).astype(BF16)
    wq = w_q_up[i].reshape(Q_LORA, H_B, NOPE_B + ROPE_B)
    w_qup = jnp.concatenate([wq[:, :, :NOPE_B].reshape(Q_LORA, H_B * NOPE_B),
                             wq[:, :, NOPE_B:].reshape(Q_LORA, H_B * ROPE_B)], axis=1).astype(BF16)
    return {
        "w_in": w_in, "w_qup": w_qup,
        "w_uk": jnp.transpose(w_uk[i], (1, 2, 0)).astype(BF16),
        "w_uv": jnp.transpose(w_uv[i], (1, 0, 2)).astype(BF16),
        "w_out": w_out_ab[i].astype(BF16),
        "g_q": _tile2(g_q_a[i]), "g_k": _tile2(g_k_a[i]),
        "g_ql": g_qlat[i].reshape(1, -1), "g_qbn": g_q_b[i, :NOPE_B].reshape(1, -1),
        "g_qbr": _tile2(g_q_b[i, NOPE_B:]), "g_kvl": g_kvlat[i].reshape(1, -1),
        "g_kr": _tile2(g_kr_b[i]),
    }


def _prep_c(i, w_in_c, g_q_c, g_k_c, w_out_c):
    d = w_in_c.shape[1]
    w_in = w_in_c[i]
    wq = w_in[:, :QC_W].reshape(d, KV_C // 2, 2, G_C, HD_C).transpose(0, 1, 3, 2, 4).reshape(d, QC_W)
    w_in = jnp.concatenate([wq, w_in[:, QC_W:]], axis=1).astype(BF16)
    wo = w_out_c[i].reshape(KV_C // 2, 2, G_C, HD_C, -1).transpose(0, 2, 1, 3, 4).reshape(QC_W, -1)
    return {"w_in": w_in, "w_out": wo.astype(BF16), "g_q": _tile2(g_q_c[i]), "g_k": _tile2(g_k_c[i])}


def kernel(x_prompt, x_sample, cache_diff_k, cache_diff_v, cache_mla_ckv, cache_mla_krope,
           state_swa_k, state_swa_v, page_table, g_mix, g_ffn, w_in_ab, g_q_a, g_k_a, lam_a,
           g_sub_a, g_qlat, w_q_up, g_q_b, g_kvlat, g_kr_b, w_uk, w_uv, w_out_ab, w_in_c,
           g_q_c, g_k_c, sinks_c, w_out_c, w_up, w_down):
    batch, seq, d = x_prompt.shape
    db = x_sample.shape[0]
    depth = g_mix.shape[0]
    n_pool, page_size = cache_diff_k.shape[1], cache_diff_k.shape[2]
    n_ab = cache_diff_k.shape[0]
    n_c = state_swa_k.shape[0]
    past_len = page_table.shape[1] * page_size
    t_p = batch * seq

    yp = x_prompt.reshape(t_p, d)
    ys = x_sample.reshape(db, d)

    pos_p = jnp.arange(seq)
    pos_s = jnp.full((db,), past_len)
    taba_p, tabb_p = _rope_tables(pos_p, DH_A // ROT_FRAC), _rope_tables(pos_p, ROPE_B)
    taba_s, tabb_s = _rope_tables(pos_s, DH_A // ROT_FRAC), _rope_tables(pos_s, ROPE_B)

    ck = jnp.transpose(cache_diff_k, (0, 1, 3, 4, 5, 2)).reshape(n_ab, n_pool, KA_W, page_size)
    cv = cache_diff_v.reshape(n_ab, n_pool, page_size * KV_A, 2 * DH_A)
    cr = jnp.transpose(cache_mla_krope, (0, 1, 3, 2))
    skt = jnp.transpose(state_swa_k, (0, 1, 3, 4, 2)).reshape(n_c, db, KC_W, WINDOW)
    svt = jnp.transpose(state_swa_v, (0, 1, 3, 4, 2)).reshape(n_c, db, KC_W, WINDOW)

    eye_kv = jnp.eye(KV_A, dtype=F32)
    eye_c = jnp.eye(2, dtype=F32)
    eye_kvc = jnp.eye(KV_C, dtype=F32)

    outs = {k: [] for k in ("dk_p", "dv_p", "ck_p", "kr_p", "sk_p", "sv_p",
                            "dk_s", "dv_s", "ck_s", "kr_s", "sk_s", "sv_s")}
    for l in range(depth):
        i = l // 2
        gm = g_mix[l].reshape(1, d)
        if l % 2 == 0:
            lam_init = 0.8 - 0.6 * math.exp(-0.3 * l)
            w = _prep_ab(i, w_in_ab, g_q_a, g_k_a, g_qlat, w_q_up, g_q_b, g_kvlat, g_kr_b, w_uk,
                         w_uv, w_out_ab)
            lam_vecs = lam_a[i]
            g_sub = g_sub_a[i].reshape(1, -1)
            q, k32, v32, c32, r32, kbf, vbf, kcat, qcat = _ab_project(yp, seq, gm, w, taba_p, tabb_p)
            oa = _diff_attn(q, kbf, vbf, lam_vecs, g_sub, lam_init, batch, seq)
            ol = _mla_attn(qcat, kcat, batch, seq)
            yp = _ab_out(oa, ol, w, yp)
            outs["dk_p"].append(k32.reshape(batch, seq, KV_A, 2, DH_A))
            outs["dv_p"].append(v32.reshape(batch, seq, KV_A, 2 * DH_A))
            outs["ck_p"].append(c32.reshape(batch, seq, KV_LORA))
            outs["kr_p"].append(r32.reshape(batch, seq, ROPE_B))
            q, k32, v32, c32, r32, _, _, _, qcat = _ab_project(ys, db, gm, w, taba_s, tabb_s)
            q5 = q.astype(F32).reshape(db, KV_A, G_A, 2, DH_A)
            qz = jnp.einsum("bkgcd,kK,cC->bckgKCd", q5, eye_kv, eye_c).reshape(db, 2 * H_A, KA_W)
            oa, ol = _dec_ab_attn(
                page_table, lam_vecs, g_sub, qz, k32.reshape(db, 1, KA_W), v32.reshape(db, 1, VA_W),
                qcat.astype(F32).reshape(db, H_B, QCAT_W), c32.reshape(db, 1, KV_LORA),
                jnp.concatenate([r32, r32], axis=1).reshape(db, 1, LANES),
                ck, cv, cache_mla_ckv, cr, i, lam_init)
            ys = _ab_out(oa.reshape(db, QA_W).astype(BF16),
                         ol.reshape(db, H_B * KV_LORA).astype(BF16), w, ys)
            outs["dk_s"].append(k32.reshape(db, 1, KV_A, 2, DH_A))
            outs["dv_s"].append(v32.reshape(db, 1, KV_A, 2 * DH_A))
            outs["ck_s"].append(c32.reshape(db, 1, KV_LORA))
            outs["kr_s"].append(r32.reshape(db, 1, ROPE_B))
        else:
            w = _prep_c(i, w_in_c, g_q_c, g_k_c, w_out_c)
            q, k32, v32, kbf, vbf = _c_project(yp, seq, gm, w, taba_p)
            a = _swa_attn(q, kbf, vbf, sinks_c[i], batch, seq)
            yp = _c_out(a, w["w_out"], yp)
            outs["sk_p"].append(k32.reshape(batch, seq, KV_C, HD_C)[:, -WINDOW:])
            outs["sv_p"].append(v32.reshape(batch, seq, KV_C, HD_C)[:, -WINDOW:])
            q, k32, v32, _, _ = _c_project(ys, db, gm, w, taba_s)
            q5 = q.astype(F32).reshape(db, KV_C // 2, G_C, 2, HD_C)
            q4 = q5.transpose(0, 1, 3, 2, 4).reshape(db, KV_C, G_C, HD_C)
            qz = jnp.einsum("bkgd,kK->bkgKd", q4, eye_kvc).reshape(db, H_C, KC_W)
            o = _dec_c_attn(sinks_c[i].reshape(H_C, 1), qz, skt, svt, k32.reshape(db, 1, KC_W),
                            v32.reshape(db, 1, KC_W), i)
            o = o.reshape(db, KV_C, G_C, KV_C, HD_C).sum(axis=3)
            a = o.reshape(db, KV_C // 2, 2, G_C, HD_C).transpose(0, 1, 3, 2, 4).reshape(db, QC_W)
            ys = _c_out(a.astype(BF16), w["w_out"], ys)
            knew = k32.reshape(db, 1, KV_C, HD_C)
            vnew = v32.reshape(db, 1, KV_C, HD_C)
            outs["sk_s"].append(jnp.concatenate([state_swa_k[i], knew], axis=1)[:, -WINDOW:])
            outs["sv_s"].append(jnp.concatenate([state_swa_v[i], vnew], axis=1)[:, -WINDOW:])
        gf = g_ffn[l].reshape(1, d)
        wu = w_up[l].astype(BF16)
        wd = w_down[l].astype(BF16)
        yp = _ffn(yp, gf, wu, wd)
        ys = _ffn(ys, gf, wu, wd)

    st = lambda name: jnp.stack(outs[name])
    return (yp.reshape(batch, seq, d), ys.reshape(db, 1, d),
            st("dk_p"), st("dv_p"), st("ck_p"), st("kr_p"), st("sk_p"), st("sv_p"),
            st("dk_s"), st("dv_s"), st("ck_s"), st("kr_s"), st("sk_s"), st("sv_s"))
```

```python
import functools
import math

import jax
import jax.numpy as jnp
from jax import lax
from jax.experimental import pallas as pl
from jax.experimental.pallas import tpu as pltpu

H_A, KV_A, DH_A = 8, 2, 64
G_A = H_A // KV_A
H_B, Q_LORA, KV_LORA, NOPE_B, ROPE_B, V_B = 8, 512, 256, 128, 64, 128
H_C, KV_C, HD_C = 32, 4, 64
G_C = H_C // KV_C
WINDOW = 128
ROPE_THETA = 500000.0
ROT_FRAC = 4
EPS = 1e-6
NEG_INF = -1e30

QA_W = H_A * 2 * DH_A
KA_W = KV_A * 2 * DH_A
VA_W = KV_A * 2 * DH_A
QCAT_W = KV_LORA + 2 * ROPE_B
IN_AB_PAD = QA_W + KA_W + VA_W + Q_LORA + KV_LORA + 2 * ROPE_B
QC_W = H_C * HD_C
KC_W = KV_C * HD_C

LANES = 128
VMEM_LIMIT = 56 * 1024 * 1024

F32 = jnp.float32
BF16 = jnp.bfloat16


def _cparams(sem):
    return pltpu.CompilerParams(dimension_semantics=sem, vmem_limit_bytes=VMEM_LIMIT)


def _const_spec(shape):
    nd = len(shape)
    return pl.BlockSpec(shape, lambda *_: (0,) * nd, pipeline_mode=pl.Buffered(1))


def _dot(a, b):
    return jnp.dot(a, b, preferred_element_type=F32)


def _dot_nt(a, b):
    return lax.dot_general(a, b, (((1,), (1,)), ((), ())), preferred_element_type=F32)


def _rms(x, g):
    ms = jnp.mean(x * x, axis=-1, keepdims=True)
    return x * lax.rsqrt(ms + EPS) * g


def _lo_mask(rows):
    return lax.broadcasted_iota(jnp.int32, (rows, LANES), 1) < (LANES // 2)


def _norm64(chunk, gain, lo):
    sq = chunk * chunk
    ss_lo = jnp.sum(jnp.where(lo, sq, 0.0), axis=-1, keepdims=True)
    ss_hi = jnp.sum(jnp.where(lo, 0.0, sq), axis=-1, keepdims=True)
    ms = jnp.where(lo, ss_lo, ss_hi) * (1.0 / 64.0)
    return chunk * lax.rsqrt(ms + EPS) * gain


def _rope(y, tab_ref, half):
    up = pltpu.roll(y, LANES - half, 1)
    down = pltpu.roll(y, half, 1)
    return y * tab_ref[0] + up * tab_ref[1] + down * tab_ref[2]


def _ab_proj_kernel(x_ref, gmix_ref, win_ref, wqup_ref, wuk_ref, gq_ref, gk_ref, gql_ref,
                    gqbn_ref, gqbr_ref, gkvl_ref, gkr_ref, taba_ref, tabb_ref,
                    q_ref, k32_ref, v32_ref, ckv32_ref, kr32_ref, kbf_ref, vbf_ref, kcat_ref,
                    qcat_ref):
    tm = x_ref.shape[0]
    lo = _lo_mask(tm)
    h = _rms(x_ref[...], gmix_ref[...]).astype(BF16)
    z = _dot(h, win_ref[...])
    scale_a = DH_A ** -0.5
    scale_b = (NOPE_B + ROPE_B) ** -0.5
    rot_a = DH_A // ROT_FRAC

    for j in range(QA_W // LANES):
        c = z[:, j * LANES:(j + 1) * LANES]
        r = _rope(_norm64(c, gq_ref[...], lo), taba_ref, rot_a // 2)
        q_ref[:, j * LANES:(j + 1) * LANES] = (r * scale_a).astype(BF16)
    for j in range(KA_W // LANES):
        c = z[:, QA_W + j * LANES:QA_W + (j + 1) * LANES]
        r = _rope(_norm64(c, gk_ref[...], lo), taba_ref, rot_a // 2)
        k32_ref[:, j * LANES:(j + 1) * LANES] = r
        kbf_ref[:, j * LANES:(j + 1) * LANES] = r.astype(BF16)
    o = QA_W + KA_W
    v = z[:, o:o + VA_W]
    v32_ref[...] = v
    vbf_ref[...] = v.astype(BF16)
    o += VA_W

    cq = _rms(z[:, o:o + Q_LORA], gql_ref[...]).astype(BF16)
    o += Q_LORA
    qb = _dot(cq, wqup_ref[...])
    nope_w = H_B * NOPE_B
    ss_n = []
    for hh in range(H_B):
        c = qb[:, hh * NOPE_B:(hh + 1) * NOPE_B]
        ss_n.append(jnp.sum(c * c, axis=-1, keepdims=True))
    inv_hd = 1.0 / (NOPE_B + ROPE_B)
    for i in range(H_B // 2):
        rc = qb[:, nope_w + i * LANES:nope_w + (i + 1) * LANES]
        sq = rc * rc
        ss_lo = jnp.sum(jnp.where(lo, sq, 0.0), axis=-1, keepdims=True)
        ss_hi = jnp.sum(jnp.where(lo, 0.0, sq), axis=-1, keepdims=True)
        r_e = lax.rsqrt((ss_n[2 * i] + ss_lo) * inv_hd + EPS)
        r_o = lax.rsqrt((ss_n[2 * i + 1] + ss_hi) * inv_hd + EPS)
        yr = rc * jnp.where(lo, r_e, r_o) * gqbr_ref[...]
        qr = _rope(yr, tabb_ref, ROPE_B // 2) * scale_b
        for hh, r_h, keep_lo in ((2 * i, r_e, True), (2 * i + 1, r_o, False)):
            qn = (qb[:, hh * NOPE_B:(hh + 1) * NOPE_B] * r_h * gqbn_ref[...]).astype(BF16)
            ql = _dot(qn, wuk_ref[hh]) * scale_b
            qcat_ref[:, hh * QCAT_W:hh * QCAT_W + KV_LORA] = ql.astype(BF16)
            tail = jnp.where(lo, qr, 0.0) if keep_lo else jnp.where(lo, 0.0, qr)
            qcat_ref[:, hh * QCAT_W + KV_LORA:(hh + 1) * QCAT_W] = tail.astype(BF16)

    ckv = _rms(z[:, o:o + KV_LORA], gkvl_ref[...])
    o += KV_LORA
    ckv32_ref[...] = ckv
    kcat_ref[:, 0:KV_LORA] = ckv.astype(BF16)
    kr = _rope(_norm64(z[:, o:o + LANES], gkr_ref[...], lo), tabb_ref, ROPE_B // 2)
    kr32_ref[...] = kr[:, 0:ROPE_B]
    kcat_ref[:, KV_LORA:KV_LORA + LANES] = kr.astype(BF16)


def _ab_project(x, pos_tiles, gmix, w, taba, tabb):
    t, d = x.shape
    tm = min(256, t)
    n_pos = pos_tiles // tm if pos_tiles >= tm else 1
    tab_spec = pl.BlockSpec((3, tm, LANES), lambda i: (0, i % n_pos, 0))
    row = lambda wdt: pl.BlockSpec((tm, wdt), lambda i: (i, 0))
    out_shapes = (
        jax.ShapeDtypeStruct((t, QA_W), BF16),
        jax.ShapeDtypeStruct((t, KA_W), F32),
        jax.ShapeDtypeStruct((t, VA_W), F32),
        jax.ShapeDtypeStruct((t, KV_LORA), F32),
        jax.ShapeDtypeStruct((t, ROPE_B), F32),
        jax.ShapeDtypeStruct((t, KA_W), BF16),
        jax.ShapeDtypeStruct((t, VA_W), BF16),
        jax.ShapeDtypeStruct((t, QCAT_W), BF16),
        jax.ShapeDtypeStruct((t, H_B * QCAT_W), BF16),
    )
    out_specs = (row(QA_W), row(KA_W), row(VA_W), row(KV_LORA), row(ROPE_B), row(KA_W),
                 row(VA_W), row(QCAT_W), row(H_B * QCAT_W))
    in_specs = [
        row(d), _const_spec((1, d)), _const_spec(w["w_in"].shape), _const_spec(w["w_qup"].shape),
        _const_spec(w["w_uk"].shape), _const_spec((1, LANES)), _const_spec((1, LANES)),
        _const_spec((1, Q_LORA)), _const_spec((1, NOPE_B)), _const_spec((1, LANES)),
        _const_spec((1, KV_LORA)), _const_spec((1, LANES)), tab_spec, tab_spec,
    ]
    return pl.pallas_call(
        _ab_proj_kernel, grid=(t // tm,), in_specs=in_specs, out_specs=out_specs,
        out_shape=out_shapes, compiler_params=_cparams(("parallel",)), name="ab_proj",
    )(x, gmix, w["w_in"], w["w_qup"], w["w_uk"], w["g_q"], w["g_k"], w["g_ql"], w["g_qbn"],
      w["g_qbr"], w["g_kvl"], w["g_kr"], taba, tabb)


def _c_proj_kernel(x_ref, gmix_ref, win_ref, gq_ref, gk_ref, taba_ref,
                   q_ref, k32_ref, v32_ref, kbf_ref, vbf_ref):
    tm = x_ref.shape[0]
    lo = _lo_mask(tm)
    h = _rms(x_ref[...], gmix_ref[...]).astype(BF16)
    z = _dot(h, win_ref[...])
    scale = HD_C ** -0.5
    half = HD_C // ROT_FRAC // 2
    for j in range(QC_W // LANES):
        c = z[:, j * LANES:(j + 1) * LANES]
        r = _rope(_norm64(c, gq_ref[...], lo), taba_ref, half)
        q_ref[:, j * LANES:(j + 1) * LANES] = (r * scale).astype(BF16)
    for j in range(KC_W // LANES):
        c = z[:, QC_W + j * LANES:QC_W + (j + 1) * LANES]
        r = _rope(_norm64(c, gk_ref[...], lo), taba_ref, half)
        k32_ref[:, j * LANES:(j + 1) * LANES] = r
        kbf_ref[:, j * LANES:(j + 1) * LANES] = r.astype(BF16)
    v = z[:, QC_W + KC_W:QC_W + 2 * KC_W]
    v32_ref[...] = v
    vbf_ref[...] = v.astype(BF16)


def _c_project(x, pos_tiles, gmix, w, taba):
    t, d = x.shape
    tm = min(256, t)
    n_pos = pos_tiles // tm if pos_tiles >= tm else 1
    tab_spec = pl.BlockSpec((3, tm, LANES), lambda i: (0, i % n_pos, 0))
    row = lambda wdt: pl.BlockSpec((tm, wdt), lambda i: (i, 0))
    out_shapes = (
        jax.ShapeDtypeStruct((t, QC_W), BF16),
        jax.ShapeDtypeStruct((t, KC_W), F32),
        jax.ShapeDtypeStruct((t, KC_W), F32),
        jax.ShapeDtypeStruct((t, KC_W), BF16),
        jax.ShapeDtypeStruct((t, KC_W), BF16),
    )
    in_specs = [row(d), _const_spec((1, d)), _const_spec(w["w_in"].shape),
                _const_spec((1, LANES)), _const_spec((1, LANES)), tab_spec]
    return pl.pallas_call(
        _c_proj_kernel, grid=(t // tm,), in_specs=in_specs,
        out_specs=(row(QC_W), row(KC_W), row(KC_W), row(KC_W), row(KC_W)),
        out_shape=out_shapes, compiler_params=_cparams(("parallel",)), name="c_proj",
    )(x, gmix, w["w_in"], w["g_q"], w["g_k"], taba)


def _ab_out_kernel(oa_ref, ol_ref, wuv_ref, wo_ref, x_ref, o_ref):
    half = H_A * 2 * DH_A
    acc = x_ref[...] + _dot(oa_ref[...], wo_ref[0:half, :])
    obs = [_dot(ol_ref[:, hh * KV_LORA:(hh + 1) * KV_LORA], wuv_ref[hh]).astype(BF16)
           for hh in range(H_B)]
    ob = jnp.concatenate(obs, axis=1)
    o_ref[...] = acc + _dot(ob, wo_ref[half:, :])


def _ab_out(oa, ol, w, x):
    t, d = x.shape
    tm = min(256, t)
    row = lambda wdt: pl.BlockSpec((tm, wdt), lambda i: (i, 0))
    return pl.pallas_call(
        _ab_out_kernel, grid=(t // tm,),
        in_specs=[row(oa.shape[1]), row(ol.shape[1]), _const_spec(w["w_uv"].shape),
                  _const_spec(w["w_out"].shape), row(d)],
        out_specs=row(d), out_shape=jax.ShapeDtypeStruct((t, d), F32),
        compiler_params=_cparams(("parallel",)), name="ab_out",
    )(oa, ol, w["w_uv"], w["w_out"], x)


def _c_out_kernel(a_ref, wo_ref, x_ref, o_ref):
    o_ref[...] = x_ref[...] + _dot(a_ref[...], wo_ref[...])


def _c_out(a, w_out, x):
    t, d = x.shape
    tm = min(256, t)
    row = lambda wdt: pl.BlockSpec((tm, wdt), lambda i: (i, 0))
    return pl.pallas_call(
        _c_out_kernel, grid=(t // tm,),
        in_specs=[row(a.shape[1]), _const_spec(w_out.shape), row(d)],
        out_specs=row(d), out_shape=jax.ShapeDtypeStruct((t, d), F32),
        compiler_params=_cparams(("parallel",)), name="c_out",
    )(a, w_out, x)


def _ffn_kernel(x_ref, g_ref, wup_ref, wdn_ref, o_ref, h_scr, acc_scr):
    j = pl.program_id(1)

    @pl.when(j == 0)
    def _():
        x = x_ref[...]
        h_scr[...] = _rms(x, g_ref[...]).astype(BF16)
        acc_scr[...] = x

    a = jnp.maximum(_dot(h_scr[...], wup_ref[...]), 0.0)
    acc_scr[...] += _dot((a * a).astype(BF16), wdn_ref[...])

    @pl.when(j == pl.num_programs(1) - 1)
    def _():
        o_ref[...] = acc_scr[...]


def _ffn(x, g, w_up, w_dn):
    t, d = x.shape
    f = w_up.shape[1]
    tm = min(512, t)
    tf = min(512, f)
    return pl.pallas_call(
        _ffn_kernel, grid=(t // tm, f // tf),
        in_specs=[pl.BlockSpec((tm, d), lambda i, j: (i, 0)), _const_spec((1, d)),
                  pl.BlockSpec((d, tf), lambda i, j: (0, j)),
                  pl.BlockSpec((tf, d), lambda i, j: (j, 0))],
        out_specs=pl.BlockSpec((tm, d), lambda i, j: (i, 0)),
        out_shape=jax.ShapeDtypeStruct((t, d), F32),
        scratch_shapes=[pltpu.VMEM((tm, d), BF16), pltpu.VMEM((tm, d), F32)],
        compiler_params=_cparams(("parallel", "arbitrary")), name="ffn",
    )(x, g, w_up, w_dn)


def _diff_lambda(lam_ref, lam_init):
    l = lam_ref[...]
    a = jnp.sum(l[0:1] * l[1:2], axis=-1, keepdims=True)
    b = jnp.sum(l[2:3] * l[3:4], axis=-1, keepdims=True)
    return jnp.exp(a) - jnp.exp(b) + lam_init


def _lanes(x, width):
    n = width // LANES
    return x if n == 1 else jnp.concatenate([x] * n, axis=1)


def _diff_attn_kernel(lam_ref, gsub_ref, q_ref, k_ref, v_ref, o_ref, m_scr, acc_scr,
                      *, lam_init, tq, tk, gpc):
    qi = pl.program_id(2)
    rows = gpc * tq
    n_ch = G_A // gpc
    m_scr[...] = jnp.full(m_scr.shape, NEG_INF, F32)
    acc_scr[...] = jnp.zeros(acc_scr.shape, F32)

    def tile(ks, width, masked):
        kt = k_ref[pl.ds(ks, width), :]
        vt = v_ref[pl.ds(ks, width), :]
        v1 = jnp.concatenate([vt, jnp.ones_like(vt)], axis=1)
        lo = _lo_mask(width)
        zero = jnp.zeros_like(kt)
        if masked:
            r = lax.broadcasted_iota(jnp.int32, (rows, width), 0) % tq
            cc = lax.broadcasted_iota(jnp.int32, (rows, width), 1)
            keep = cc <= r
        for c in range(2):
            kc = jnp.where(lo, kt, zero) if c == 0 else jnp.where(lo, zero, kt)
            for ch in range(n_ch):
                qs = [q_ref[:, (ch * gpc + u) * LANES:(ch * gpc + u + 1) * LANES] for u in range(gpc)]
                q = qs[0] if gpc == 1 else jnp.concatenate(qs, axis=0)
                s = _dot_nt(q, kc)
                if masked:
                    s = jnp.where(keep, s, NEG_INF)
                m_prev = m_scr[c, ch]
                m_new = jnp.maximum(m_prev, jnp.max(s, axis=-1, keepdims=True))
                alpha = jnp.exp(m_prev - m_new)
                p = jnp.exp(s - _lanes(m_new, width))
                acc_scr[c, ch] = _lanes(alpha, 2 * LANES) * acc_scr[c, ch] + _dot(p.astype(BF16), v1)
                m_scr[c, ch] = m_new

    def body(ki, carry):
        tile(pl.multiple_of(ki * tk, tk), tk, False)
        return carry

    n_full = (qi * tq) // tk
    lax.fori_loop(0, n_full, body, 0)
    if tk > tq:
        @pl.when((qi * tq) % tk != 0)
        def _():
            tile(pl.multiple_of(n_full * tk, tq), tq, False)
    tile(pl.multiple_of(qi * tq, tq), tq, True)

    lam = _diff_lambda(lam_ref, lam_init)
    for ch in range(n_ch):
        a0 = acc_scr[0, ch]
        a1 = acc_scr[1, ch]
        out = a0[:, 0:LANES] / a0[:, LANES:2 * LANES] - lam * (a1[:, 0:LANES] / a1[:, LANES:2 * LANES])
        out = _rms(out, gsub_ref[...]) * (1.0 - lam_init)
        for u in range(gpc):
            g = ch * gpc + u
            o_ref[:, g * LANES:(g + 1) * LANES] = out[u * tq:(u + 1) * tq].astype(BF16)


def _diff_attn(q, k, v, lam_vecs, g_sub, lam_init, batch, seq):
    tq = min(256, seq)
    tk = min(512, seq)
    gpc = 1
    nq = seq // tq
    gw = G_A * LANES
    kern = functools.partial(_diff_attn_kernel, lam_init=lam_init, tq=tq, tk=tk, gpc=gpc)
    return pl.pallas_call(
        kern, grid=(batch, KV_A, nq),
        in_specs=[_const_spec(lam_vecs.shape), _const_spec(g_sub.shape),
                  pl.BlockSpec((tq, gw), lambda b, kv, i: (b * nq + i, kv)),
                  pl.BlockSpec((seq, LANES), lambda b, kv, i: (b, kv)),
                  pl.BlockSpec((seq, LANES), lambda b, kv, i: (b, kv))],
        out_specs=pl.BlockSpec((tq, gw), lambda b, kv, i: (b * nq + i, kv)),
        out_shape=jax.ShapeDtypeStruct(q.shape, BF16),
        scratch_shapes=[pltpu.VMEM((2, G_A // gpc, gpc * tq, LANES), F32),
                        pltpu.VMEM((2, G_A // gpc, gpc * tq, 2 * LANES), F32)],
        compiler_params=_cparams(("parallel", "parallel", "arbitrary")), name="diff_attn",
    )(lam_vecs, g_sub, q, k, v)


def _mla_attn_kernel(q_ref, k_ref, o_ref, m_scr, l_scr, acc_scr, *, tq, tk, hpc):
    qi = pl.program_id(1)
    m_scr[...] = jnp.full(m_scr.shape, NEG_INF, F32)
    l_scr[...] = jnp.zeros(l_scr.shape, F32)
    acc_scr[...] = jnp.zeros(acc_scr.shape, F32)
    rows = hpc * tq

    def tile(ks, width, masked):
        kt = k_ref[pl.ds(ks, width), :]
        vt = kt[:, 0:KV_LORA]
        if masked:
            r = lax.broadcasted_iota(jnp.int32, (rows, width), 0) % tq
            cc = lax.broadcasted_iota(jnp.int32, (rows, width), 1)
            keep = cc <= r
        for ch in range(H_B // hpc):
            qs = [q_ref[:, (ch * hpc + u) * QCAT_W:(ch * hpc + u + 1) * QCAT_W] for u in range(hpc)]
            q = qs[0] if hpc == 1 else jnp.concatenate(qs, axis=0)
            s = _dot_nt(q, kt)
            if masked:
                s = jnp.where(keep, s, NEG_INF)
            m_prev = m_scr[ch]
            m_new = jnp.maximum(m_prev, jnp.max(s, axis=-1, keepdims=True))
            alpha = jnp.exp(m_prev - m_new)
            p = jnp.exp(s - _lanes(m_new, width))
            l_scr[ch] = alpha * l_scr[ch] + jnp.sum(p, axis=-1, keepdims=True)
            acc_scr[ch] = _lanes(alpha, KV_LORA) * acc_scr[ch] + _dot(p.astype(BF16), vt)
            m_scr[ch] = m_new

    def body(ki, carry):
        tile(pl.multiple_of(ki * tk, tk), tk, False)
        return carry

    n_full = (qi * tq) // tk
    lax.fori_loop(0, n_full, body, 0)
    if tk > tq:
        @pl.when((qi * tq) % tk != 0)
        def _():
            tile(pl.multiple_of(n_full * tk, tq), tq, False)
    tile(pl.multiple_of(qi * tq, tq), tq, True)
    for ch in range(H_B // hpc):
        out = acc_scr[ch] / _lanes(l_scr[ch], KV_LORA)
        for u in range(hpc):
            hh = ch * hpc + u
            o_ref[:, hh * KV_LORA:(hh + 1) * KV_LORA] = out[u * tq:(u + 1) * tq].astype(BF16)


def _mla_attn(qcat, kcat, batch, seq):
    tq = min(256, seq)
    tk = min(512, seq)
    nq = seq // tq
    hpc = 4
    kern = functools.partial(_mla_attn_kernel, tq=tq, tk=tk, hpc=hpc)
    return pl.pallas_call(
        kern, grid=(batch, nq),
        in_specs=[pl.BlockSpec((tq, H_B * QCAT_W), lambda b, i: (b * nq + i, 0)),
                  pl.BlockSpec((seq, QCAT_W), lambda b, i: (b, 0))],
        out_specs=pl.BlockSpec((tq, H_B * KV_LORA), lambda b, i: (b * nq + i, 0)),
        out_shape=jax.ShapeDtypeStruct((batch * seq, H_B * KV_LORA), BF16),
        scratch_shapes=[pltpu.VMEM((H_B // hpc, hpc * tq, LANES), F32), pltpu.VMEM((H_B // hpc, hpc * tq, LANES), F32),
                        pltpu.VMEM((H_B // hpc, hpc * tq, KV_LORA), F32)],
        compiler_params=_cparams(("parallel", "arbitrary")), name="mla_attn",
    )(qcat, kcat)


def _swa_attn_kernel(sink_ref, q_ref, kc_ref, kp_ref, vc_ref, vp_ref, o_ref, *, gpc):
    j = pl.program_id(1)
    w = WINDOW
    keys = jnp.concatenate([kp_ref[...], kc_ref[...]], axis=0)
    vals = jnp.concatenate([vp_ref[...], vc_ref[...]], axis=0)
    rows = gpc * w
    r = lax.broadcasted_iota(jnp.int32, (rows, 2 * w), 0) % w
    c = lax.broadcasted_iota(jnp.int32, (rows, 2 * w), 1)
    allowed = (c >= r) & (c <= r + w) & ((j > 0) | (c >= w))
    lo = _lo_mask(2 * w)
    for kvp in range(KV_C // 2):
        kk = keys[:, kvp * LANES:(kvp + 1) * LANES]
        vv = vals[:, kvp * LANES:(kvp + 1) * LANES]
        zero = jnp.zeros_like(kk)
        one = jnp.ones_like(vv)
        ksel = (jnp.where(lo, kk, zero), jnp.where(lo, zero, kk))
        vsel = (jnp.concatenate([jnp.where(lo, vv, zero), one], axis=1),
                jnp.concatenate([jnp.where(lo, zero, vv), one], axis=1))
        for gc in range(G_C // gpc):
            g0 = gc * gpc
            qs = [q_ref[:, (kvp * G_C + g0 + u) * LANES:(kvp * G_C + g0 + u + 1) * LANES]
                  for u in range(gpc)]
            q = qs[0] if gpc == 1 else jnp.concatenate(qs, axis=0)
            out = None
            for par in range(2):
                kv = 2 * kvp + par
                sinks = [jnp.full((w, LANES), sink_ref[kv * G_C + g0 + u], F32) for u in range(gpc)]
                sink = sinks[0] if gpc == 1 else jnp.concatenate(sinks, axis=0)
                s = jnp.where(allowed, _dot_nt(q, ksel[par]), NEG_INF)
                m = jnp.maximum(jnp.max(s, axis=-1, keepdims=True), sink)
                e = jnp.exp(s - _lanes(m, 2 * w))
                acc = _dot(e.astype(BF16), vsel[par])
                o = acc[:, 0:LANES] / (acc[:, LANES:2 * LANES] + jnp.exp(sink - m))
                out = o if out is None else out + o
            for u in range(gpc):
                col = (kvp * G_C + g0 + u) * LANES
                o_ref[:, col:col + LANES] = out[u * w:(u + 1) * w].astype(BF16)


def _swa_attn(q, k, v, sinks, batch, seq):
    nb = seq // WINDOW
    cur = lambda b, j: (b * nb + j, 0)
    prev = lambda b, j: (b * nb + jnp.maximum(j - 1, 0), 0)
    return pl.pallas_call(
        functools.partial(_swa_attn_kernel, gpc=2), grid=(batch, nb),
        in_specs=[pl.BlockSpec(memory_space=pltpu.SMEM),
                  pl.BlockSpec((WINDOW, QC_W), cur),
                  pl.BlockSpec((WINDOW, KC_W), cur), pl.BlockSpec((WINDOW, KC_W), prev),
                  pl.BlockSpec((WINDOW, KC_W), cur), pl.BlockSpec((WINDOW, KC_W), prev)],
        out_specs=pl.BlockSpec((WINDOW, QC_W), cur),
        out_shape=jax.ShapeDtypeStruct(q.shape, BF16),
        compiler_params=_cparams(("parallel", "parallel")), name="swa_attn",
    )(sinks, q, k, k, v, v)


PAGES_PER_CHUNK = 16


def _dec_ab_kernel(pt_ref, lam_ref, gsub_ref, qz_ref, knew_ref, vnew_ref, qc_ref, cnew_ref,
                   rnew_ref, kc_hbm, vc_hbm, cc_hbm, rc_hbm, oa_ref, ol_ref,
                   kbuf, vbuf, cbuf, rbuf, sem, *, layer, lam_init, n_pages):
    b = pl.program_id(0)
    pp = kbuf.shape[1]
    n_chunks = n_pages // pp
    page_size = kbuf.shape[3]
    streams = ((kc_hbm, kbuf, 0), (vc_hbm, vbuf, KV_A * page_size), (cc_hbm, cbuf, page_size),
               (rc_hbm, rbuf, 0))

    def copies(seq_idx, chunk, slot):
        out = []
        for j in range(pp):
            page = pt_ref[seq_idx, chunk * pp + j]
            for si, (src, dst, flat_rows) in enumerate(streams):
                d = dst.at[slot, pl.ds(j * flat_rows, flat_rows)] if flat_rows else dst.at[slot, j]
                out.append(pltpu.make_async_copy(src.at[layer, page], d, sem.at[si, slot]))
        return out

    def start(seq_idx, chunk, slot):
        for cp in copies(seq_idx, chunk, slot):
            cp.start()

    def wait(seq_idx, chunk, slot):
        for cp in copies(seq_idx, chunk, slot):
            cp.wait()

    qz = qz_ref[0].astype(BF16)
    qc = qc_ref[0]
    ql = qc[:, 0:KV_LORA].astype(BF16)
    qt = qc[:, KV_LORA:QCAT_W].astype(BF16)
    row_kv0 = (lax.broadcasted_iota(jnp.int32, (2 * H_A, LANES), 0) // G_A) % KV_A == 0

    knew = knew_ref[0]
    vnew = vnew_ref[0]
    m_a = jnp.sum(qz_ref[0] * knew, axis=-1, keepdims=True)
    l_a = jnp.ones_like(m_a)
    acc_a = jnp.where(row_kv0, vnew[:, 0:LANES], vnew[:, LANES:2 * LANES])
    cnew = cnew_ref[0]
    rnew = rnew_ref[0]
    m_b = (jnp.sum(qc[:, 0:KV_LORA] * cnew, axis=-1, keepdims=True)
           + jnp.sum(qc[:, KV_LORA:QCAT_W] * rnew, axis=-1, keepdims=True))
    l_b = jnp.ones_like(m_b)
    acc_b = jnp.broadcast_to(cnew, (H_B, KV_LORA))

    @pl.when(b == 0)
    def _():
        start(0, 0, 0)

    def body(ci, carry):
        m_a, l_a, acc_a, m_b, l_b, acc_b = carry
        slot = (b * n_chunks + ci) % 2

        @pl.when(ci + 1 < n_chunks)
        def _():
            start(b, ci + 1, 1 - slot)

        @pl.when((ci + 1 == n_chunks) & (b + 1 < pl.num_programs(0)))
        def _():
            start(b + 1, 0, 1 - slot)

        wait(b, ci, slot)

        k_all = jnp.concatenate([kbuf[slot, j] for j in range(pp)], axis=1).astype(BF16)
        s = _dot(qz, k_all)
        m_new = jnp.maximum(m_a, jnp.max(s, axis=-1, keepdims=True))
        alpha = jnp.exp(m_a - m_new)
        p = jnp.exp(s - m_new)
        l_a = alpha * l_a + jnp.sum(p, axis=-1, keepdims=True)
        pb = p.astype(BF16)
        v0 = vbuf[slot, pl.ds(0, pp * page_size, stride=KV_A), :].astype(BF16)
        v1 = vbuf[slot, pl.ds(1, pp * page_size, stride=KV_A), :].astype(BF16)
        acc_a = alpha * acc_a + jnp.where(row_kv0, _dot(pb, v0), _dot(pb, v1))
        m_a = m_new

        c_all = cbuf[slot].astype(BF16)
        r_all = jnp.concatenate([rbuf[slot, j] for j in range(pp)], axis=1).astype(BF16)
        s = _dot_nt(ql, c_all) + _dot(qt, jnp.concatenate([r_all, r_all], axis=0))
        m_new = jnp.maximum(m_b, jnp.max(s, axis=-1, keepdims=True))
        alpha = jnp.exp(m_b - m_new)
        p = jnp.exp(s - m_new)
        l_b = alpha * l_b + jnp.sum(p, axis=-1, keepdims=True)
        acc_b = alpha * acc_b + _dot(p.astype(BF16), c_all)
        return m_a, l_a, acc_a, m_new, l_b, acc_b

    m_a, l_a, acc_a, m_b, l_b, acc_b = lax.fori_loop(
        0, n_chunks, body, (m_a, l_a, acc_a, m_b, l_b, acc_b))

    out = acc_a / l_a
    lam = _diff_lambda(lam_ref, lam_init)
    oa = out[0:H_A] - lam * out[H_A:2 * H_A]
    oa_ref[0] = _rms(oa, gsub_ref[...]) * (1.0 - lam_init)
    ol_ref[0] = acc_b / l_b


def _dec_ab_attn(page_table, lam_vecs, g_sub, qz, knew, vnew, qc, cnew, rnew,
                 cache_k, cache_v, cache_c, cache_r, layer, lam_init):
    db, n_pages = page_table.shape
    pp = min(PAGES_PER_CHUNK, n_pages)
    kern = functools.partial(_dec_ab_kernel, layer=layer, lam_init=lam_init, n_pages=n_pages)
    seq_spec = lambda shape: pl.BlockSpec((1,) + shape, lambda b, pt: (b, 0, 0))
    anyspec = pl.BlockSpec(memory_space=pl.ANY)
    grid_spec = pltpu.PrefetchScalarGridSpec(
        num_scalar_prefetch=1, grid=(db,),
        in_specs=[pl.BlockSpec(lam_vecs.shape, lambda b, pt: (0, 0), pipeline_mode=pl.Buffered(1)),
                  pl.BlockSpec(g_sub.shape, lambda b, pt: (0, 0), pipeline_mode=pl.Buffered(1)),
                  seq_spec(qz.shape[1:]), seq_spec(knew.shape[1:]), seq_spec(vnew.shape[1:]),
                  seq_spec(qc.shape[1:]), seq_spec(cnew.shape[1:]), seq_spec(rnew.shape[1:]),
                  anyspec, anyspec, anyspec, anyspec],
        out_specs=[seq_spec((H_A, LANES)), seq_spec((H_B, KV_LORA))],
        scratch_shapes=[pltpu.VMEM((2, pp) + cache_k.shape[2:], F32),
                        pltpu.VMEM((2, pp * cache_v.shape[2], cache_v.shape[3]), F32),
                        pltpu.VMEM((2, pp * cache_c.shape[2], cache_c.shape[3]), F32),
                        pltpu.VMEM((2, pp) + cache_r.shape[2:], F32),
                        pltpu.SemaphoreType.DMA((4, 2))])
    return pl.pallas_call(
        kern, grid_spec=grid_spec,
        out_shape=(jax.ShapeDtypeStruct((db, H_A, LANES), F32),
                   jax.ShapeDtypeStruct((db, H_B, KV_LORA), F32)),
        compiler_params=_cparams(("arbitrary",)), name="dec_ab_attn",
    )(page_table, lam_vecs, g_sub, qz, knew, vnew, qc, cnew, rnew, cache_k, cache_v, cache_c,
      cache_r)


def _dec_c_kernel(sink_ref, qz_ref, kt_ref, vt_ref, knew_ref, vnew_ref, o_ref):
    qf = qz_ref[0]
    kt = kt_ref[0, 0].astype(BF16)
    vt = vt_ref[0, 0].astype(BF16)
    sink = sink_ref[...]
    s = _dot(qf.astype(BF16), kt)
    s_new = jnp.sum(qf * knew_ref[0], axis=-1, keepdims=True)
    m = jnp.maximum(jnp.maximum(jnp.max(s, axis=-1, keepdims=True), s_new), sink)
    e = jnp.exp(s - m)
    e_new = jnp.exp(s_new - m)
    denom = jnp.sum(e, axis=-1, keepdims=True) + e_new + jnp.exp(sink - m)
    o = (_dot_nt(e.astype(BF16), vt) + e_new * vnew_ref[0]) / denom
    rkv = lax.broadcasted_iota(jnp.int32, o.shape, 0) // G_C
    ckv = lax.broadcasted_iota(jnp.int32, o.shape, 1) // HD_C
    o_ref[0] = jnp.where(rkv == ckv, o, 0.0)


def _dec_c_attn(sink_col, qz, state_kt, state_vt, knew, vnew, layer):
    db = qz.shape[0]
    w = state_kt.shape[-1]
    seq_spec = lambda shape: pl.BlockSpec((1,) + shape, lambda b: (b, 0, 0))
    st_spec = pl.BlockSpec((1, 1, KC_W, w), lambda b: (layer, b, 0, 0))
    return pl.pallas_call(
        _dec_c_kernel, grid=(db,),
        in_specs=[_const_spec(sink_col.shape), seq_spec(qz.shape[1:]), st_spec, st_spec,
                  seq_spec(knew.shape[1:]), seq_spec(vnew.shape[1:])],
        out_specs=seq_spec((H_C, KC_W)),
        out_shape=jax.ShapeDtypeStruct((db, H_C, KC_W), F32),
        compiler_params=_cparams(("parallel",)), name="dec_c_attn",
    )(sink_col, qz, state_kt, state_vt, knew, vnew)


def _rope_tables(pos, rot):
    half = rot // 2
    inv = ROPE_THETA ** (-jnp.arange(half, dtype=F32) / half)
    ang = pos.astype(F32)[:, None] * inv[None, :]
    cos, sin = jnp.cos(ang), jnp.sin(ang)
    jj = jnp.arange(LANES) % 64
    idx = jj % half
    c = jnp.where(jj < rot, cos[:, idx], 1.0)
    s1 = jnp.where(jj < half, -sin[:, idx], 0.0)
    s2 = jnp.where((jj >= half) & (jj < rot), sin[:, idx], 0.0)
    return jnp.stack([c, s1, s2]).astype(F32)


def _tile2(g):
    return jnp.concatenate([g, g]).reshape(1, -1).astype(F32)


def _prep_ab(i, w_in_ab, g_q_a, g_k_a, g_qlat, w_q_up, g_q_b, g_kvlat, g_kr_b, w_uk, w_uv,
             w_out_ab):
    w_in = w_in_ab[i]
    kr_cols = w_in[:, -ROPE_B:]
    w_in = jnp.concatenate([w_in, kr_cols], axis=1).astype(BF16)
    wq = w_q_up[i].reshape(Q_LORA, H_B, NOPE_B + ROPE_B)
    w_qup = jnp.concatenate([wq[:, :, :NOPE_B].reshape(Q_LORA, H_B * NOPE_B),
                             wq[:, :, NOPE_B:].reshape(Q_LORA, H_B * ROPE_B)], axis=1).astype(BF16)
    return {
        "w_in": w_in, "w_qup": w_qup,
        "w_uk": jnp.transpose(w_uk[i], (1, 2, 0)).astype(BF16),
        "w_uv": jnp.transpose(w_uv[i], (1, 0, 2)).astype(BF16),
        "w_out": w_out_ab[i].astype(BF16),
        "g_q": _tile2(g_q_a[i]), "g_k": _tile2(g_k_a[i]),
        "g_ql": g_qlat[i].reshape(1, -1), "g_qbn": g_q_b[i, :NOPE_B].reshape(1, -1),
        "g_qbr": _tile2(g_q_b[i, NOPE_B:]), "g_kvl": g_kvlat[i].reshape(1, -1),
        "g_kr": _tile2(g_kr_b[i]),
    }


def _prep_c(i, w_in_c, g_q_c, g_k_c, w_out_c):
    d = w_in_c.shape[1]
    w_in = w_in_c[i]
    wq = w_in[:, :QC_W].reshape(d, KV_C // 2, 2, G_C, HD_C).transpose(0, 1, 3, 2, 4).reshape(d, QC_W)
    w_in = jnp.concatenate([wq, w_in[:, QC_W:]], axis=1).astype(BF16)
    wo = w_out_c[i].reshape(KV_C // 2, 2, G_C, HD_C, -1).transpose(0, 2, 1, 3, 4).reshape(QC_W, -1)
    return {"w_in": w_in, "w_out": wo.astype(BF16), "g_q": _tile2(g_q_c[i]), "g_k": _tile2(g_k_c[i])}


def kernel(x_prompt, x_sample, cache_diff_k, cache_diff_v, cache_mla_ckv, cache_mla_krope,
           state_swa_k, state_swa_v, page_table, g_mix, g_ffn, w_in_ab, g_q_a, g_k_a, lam_a,
           g_sub_a, g_qlat, w_q_up, g_q_b, g_kvlat, g_kr_b, w_uk, w_uv, w_out_ab, w_in_c,
           g_q_c, g_k_c, sinks_c, w_out_c, w_up, w_down):
    batch, seq, d = x_prompt.shape
    db = x_sample.shape[0]
    depth = g_mix.shape[0]
    n_pool, page_size = cache_diff_k.shape[1], cache_diff_k.shape[2]
    n_ab = cache_diff_k.shape[0]
    n_c = state_swa_k.shape[0]
    past_len = page_table.shape[1] * page_size
    t_p = batch * seq

    yp = x_prompt.reshape(t_p, d)
    ys = x_sample.reshape(db, d)

    pos_p = jnp.arange(seq)
    pos_s = jnp.full((db,), past_len)
    taba_p, tabb_p = _rope_tables(pos_p, DH_A // ROT_FRAC), _rope_tables(pos_p, ROPE_B)
    taba_s, tabb_s = _rope_tables(pos_s, DH_A // ROT_FRAC), _rope_tables(pos_s, ROPE_B)

    ck = jnp.transpose(cache_diff_k, (0, 1, 3, 4, 5, 2)).reshape(n_ab, n_pool, KA_W, page_size)
    cv = cache_diff_v.reshape(n_ab, n_pool, page_size * KV_A, 2 * DH_A)
    cr = jnp.transpose(cache_mla_krope, (0, 1, 3, 2))
    skt = jnp.transpose(state_swa_k, (0, 1, 3, 4, 2)).reshape(n_c, db, KC_W, WINDOW)
    svt = jnp.transpose(state_swa_v, (0, 1, 3, 4, 2)).reshape(n_c, db, KC_W, WINDOW)

    eye_kv = jnp.eye(KV_A, dtype=F32)
    eye_c = jnp.eye(2, dtype=F32)
    eye_kvc = jnp.eye(KV_C, dtype=F32)

    outs = {k: [] for k in ("dk_p", "dv_p", "ck_p", "kr_p", "sk_p", "sv_p",
                            "dk_s", "dv_s", "ck_s", "kr_s", "sk_s", "sv_s")}
    for l in range(depth):
        i = l // 2
        gm = g_mix[l].reshape(1, d)
        if l % 2 == 0:
            lam_init = 0.8 - 0.6 * math.exp(-0.3 * l)
            w = _prep_ab(i, w_in_ab, g_q_a, g_k_a, g_qlat, w_q_up, g_q_b, g_kvlat, g_kr_b, w_uk,
                         w_uv, w_out_ab)
            lam_vecs = lam_a[i]
            g_sub = g_sub_a[i].reshape(1, -1)
            q, k32, v32, c32, r32, kbf, vbf, kcat, qcat = _ab_project(yp, seq, gm, w, taba_p, tabb_p)
            oa = _diff_attn(q, kbf, vbf, lam_vecs, g_sub, lam_init, batch, seq)
            ol = _mla_attn(qcat, kcat, batch, seq)
            yp = _ab_out(oa, ol, w, yp)
            outs["dk_p"].append(k32.reshape(batch, seq, KV_A, 2, DH_A))
            outs["dv_p"].append(v32.reshape(batch, seq, KV_A, 2 * DH_A))
            outs["ck_p"].append(c32.reshape(batch, seq, KV_LORA))
            outs["kr_p"].append(r32.reshape(batch, seq, ROPE_B))
            q, k32, v32, c32, r32, _, _, _, qcat = _ab_project(ys, db, gm, w, taba_s, tabb_s)
            q5 = q.astype(F32).reshape(db, KV_A, G_A, 2, DH_A)
            qz = jnp.einsum("bkgcd,kK,cC->bckgKCd", q5, eye_kv, eye_c).reshape(db, 2 * H_A, KA_W)
            oa, ol = _dec_ab_attn(
                page_table, lam_vecs, g_sub, qz, k32.reshape(db, 1, KA_W), v32.reshape(db, 1, VA_W),
                qcat.astype(F32).reshape(db, H_B, QCAT_W), c32.reshape(db, 1, KV_LORA),
                jnp.concatenate([r32, r32], axis=1).reshape(db, 1, LANES),
                ck, cv, cache_mla_ckv, cr, i, lam_init)
            ys = _ab_out(oa.reshape(db, QA_W).astype(BF16),
                         ol.reshape(db, H_B * KV_LORA).astype(BF16), w, ys)
            outs["dk_s"].append(k32.reshape(db, 1, KV_A, 2, DH_A))
            outs["dv_s"].append(v32.reshape(db, 1, KV_A, 2 * DH_A))
            outs["ck_s"].append(c32.reshape(db, 1, KV_LORA))
            outs["kr_s"].append(r32.reshape(db, 1, ROPE_B))
        else:
            w = _prep_c(i, w_in_c, g_q_c, g_k_c, w_out_c)
            q, k32, v32, kbf, vbf = _c_project(yp, seq, gm, w, taba_p)
            a = _swa_attn(q, kbf, vbf, sinks_c[i], batch, seq)
            yp = _c_out(a, w["w_out"], yp)
            outs["sk_p"].append(k32.reshape(batch, seq, KV_C, HD_C)[:, -WINDOW:])
            outs["sv_p"].append(v32.reshape(batch, seq, KV_C, HD_C)[:, -WINDOW:])
            q, k32, v32, _, _ = _c_project(ys, db, gm, w, taba_s)
            q5 = q.astype(F32).reshape(db, KV_C // 2, G_C, 2, HD_C)
            q4 = q5.transpose(0, 1, 3, 2, 4).reshape(db, KV_C, G_C, HD_C)
            qz = jnp.einsum("bkgd,kK->bkgKd", q4, eye_kvc).reshape(db, H_C, KC_W)
            o = _dec_c_attn(sinks_c[i].reshape(H_C, 1), qz, skt, svt, k32.reshape(db, 1, KC_W),
                            v32.reshape(db, 1, KC_W), i)
            o = o.reshape(db, KV_C, G_C, KV_C, HD_C).sum(axis=3)
            a = o.reshape(db, KV_C // 2, 2, G_C, HD_C).transpose(0, 1, 3, 2, 4).reshape(db, QC_W)
            ys = _c_out(a.astype(BF16), w["w_out"], ys)
            knew = k32.reshape(db, 1, KV_C, HD_C)
            vnew = v32.reshape(db, 1, KV_C, HD_C)
            outs["sk_s"].append(jnp.concatenate([state_swa_k[i], knew], axis=1)[:, -WINDOW:])
            outs["sv_s"].append(jnp.concatenate([state_swa_v[i], vnew], axis=1)[:, -WINDOW:])
        gf = g_ffn[l].reshape(1, d)
        wu = w_up[l].astype(BF16)
        wd = w_down[l].astype(BF16)
        yp = _ffn(yp, gf, wu, wd)
        ys = _ffn(ys, gf, wu, wd)

    st = lambda name: jnp.stack(outs[name])
    return (yp.reshape(batch, seq, d), ys.reshape(db, 1, d),
            st("dk_p"), st("dv_p"), st("ck_p"), st("kr_p"), st("sk_p"), st("sv_p"),
            st("dk_s"), st("dv_s"), st("ck_s"), st("kr_s"), st("sk_s"), st("sv_s"))
```

```python
import functools
import math

import jax
import jax.numpy as jnp
from jax import lax
from jax.experimental import pallas as pl
from jax.experimental.pallas import tpu as pltpu

H_A, KV_A, DH_A = 8, 2, 64
G_A = H_A // KV_A
H_B, Q_LORA, KV_LORA, NOPE_B, ROPE_B, V_B = 8, 512, 256, 128, 64, 128
H_C, KV_C, HD_C = 32, 4, 64
G_C = H_C // KV_C
WINDOW = 128
ROPE_THETA = 500000.0
ROT_FRAC = 4
EPS = 1e-6
NEG_INF = -1e30

QA_W = H_A * 2 * DH_A
KA_W = KV_A * 2 * DH_A
VA_W = KV_A * 2 * DH_A
QCAT_W = KV_LORA + 2 * ROPE_B
IN_AB_PAD = QA_W + KA_W + VA_W + Q_LORA + KV_LORA + 2 * ROPE_B
QC_W = H_C * HD_C
KC_W = KV_C * HD_C

LANES = 128
VMEM_LIMIT = 56 * 1024 * 1024

F32 = jnp.float32
BF16 = jnp.bfloat16


def _cparams(sem):
    return pltpu.CompilerParams(dimension_semantics=sem, vmem_limit_bytes=VMEM_LIMIT)


def _const_spec(shape):
    nd = len(shape)
    return pl.BlockSpec(shape, lambda *_: (0,) * nd, pipeline_mode=pl.Buffered(1))


def _dot(a, b):
    return jnp.dot(a, b, preferred_element_type=F32)


def _dot_nt(a, b):
    return lax.dot_general(a, b, (((1,), (1,)), ((), ())), preferred_element_type=F32)


def _rms(x, g):
    ms = jnp.mean(x * x, axis=-1, keepdims=True)
    return x * lax.rsqrt(ms + EPS) * g


def _lo_mask(rows):
    return lax.broadcasted_iota(jnp.int32, (rows, LANES), 1) < (LANES // 2)


def _norm64(chunk, gain, lo):
    sq = chunk * chunk
    ss_lo = jnp.sum(jnp.where(lo, sq, 0.0), axis=-1, keepdims=True)
    ss_hi = jnp.sum(jnp.where(lo, 0.0, sq), axis=-1, keepdims=True)
    ms = jnp.where(lo, ss_lo, ss_hi) * (1.0 / 64.0)
    return chunk * lax.rsqrt(ms + EPS) * gain


def _rope(y, tab_ref, half):
    up = pltpu.roll(y, LANES - half, 1)
    down = pltpu.roll(y, half, 1)
    return y * tab_ref[0] + up * tab_ref[1] + down * tab_ref[2]


def _ab_proj_kernel(x_ref, gmix_ref, win_ref, wqup_ref, wuk_ref, gq_ref, gk_ref, gql_ref,
                    gqbn_ref, gqbr_ref, gkvl_ref, gkr_ref, taba_ref, tabb_ref,
                    q_ref, k32_ref, v32_ref, ckv32_ref, kr32_ref, kbf_ref, vbf_ref, kcat_ref,
                    qcat_ref):
    tm = x_ref.shape[0]
    lo = _lo_mask(tm)
    h = _rms(x_ref[...], gmix_ref[...]).astype(BF16)
    z = _dot(h, win_ref[...])
    scale_a = DH_A ** -0.5
    scale_b = (NOPE_B + ROPE_B) ** -0.5
    rot_a = DH_A // ROT_FRAC

    for j in range(QA_W // LANES):
        c = z[:, j * LANES:(j + 1) * LANES]
        r = _rope(_norm64(c, gq_ref[...], lo), taba_ref, rot_a // 2)
        q_ref[:, j * LANES:(j + 1) * LANES] = (r * scale_a).astype(BF16)
    for j in range(KA_W // LANES):
        c = z[:, QA_W + j * LANES:QA_W + (j + 1) * LANES]
        r = _rope(_norm64(c, gk_ref[...], lo), taba_ref, rot_a // 2)
        k32_ref[:, j * LANES:(j + 1) * LANES] = r
        kbf_ref[:, j * LANES:(j + 1) * LANES] = r.astype(BF16)
    o = QA_W + KA_W
    v = z[:, o:o + VA_W]
    v32_ref[...] = v
    vbf_ref[...] = v.astype(BF16)
    o += VA_W

    cq = _rms(z[:, o:o + Q_LORA], gql_ref[...]).astype(BF16)
    o += Q_LORA
    qb = _dot(cq, wqup_ref[...])
    nope_w = H_B * NOPE_B
    ss_n = []
    for hh in range(H_B):
        c = qb[:, hh * NOPE_B:(hh + 1) * NOPE_B]
        ss_n.append(jnp.sum(c * c, axis=-1, keepdims=True))
    inv_hd = 1.0 / (NOPE_B + ROPE_B)
    for i in range(H_B // 2):
        rc = qb[:, nope_w + i * LANES:nope_w + (i + 1) * LANES]
        sq = rc * rc
        ss_lo = jnp.sum(jnp.where(lo, sq, 0.0), axis=-1, keepdims=True)
        ss_hi = jnp.sum(jnp.where(lo, 0.0, sq), axis=-1, keepdims=True)
        r_e = lax.rsqrt((ss_n[2 * i] + ss_lo) * inv_hd + EPS)
        r_o = lax.rsqrt((ss_n[2 * i + 1] + ss_hi) * inv_hd + EPS)
        yr = rc * jnp.where(lo, r_e, r_o) * gqbr_ref[...]
        qr = _rope(yr, tabb_ref, ROPE_B // 2) * scale_b
        for hh, r_h, keep_lo in ((2 * i, r_e, True), (2 * i + 1, r_o, False)):
            qn = (qb[:, hh * NOPE_B:(hh + 1) * NOPE_B] * r_h * gqbn_ref[...]).astype(BF16)
            ql = _dot(qn, wuk_ref[hh]) * scale_b
            qcat_ref[:, hh * QCAT_W:hh * QCAT_W + KV_LORA] = ql.astype(BF16)
            tail = jnp.where(lo, qr, 0.0) if keep_lo else jnp.where(lo, 0.0, qr)
            qcat_ref[:, hh * QCAT_W + KV_LORA:(hh + 1) * QCAT_W] = tail.astype(BF16)

    ckv = _rms(z[:, o:o + KV_LORA], gkvl_ref[...])
    o += KV_LORA
    ckv32_ref[...] = ckv
    kcat_ref[:, 0:KV_LORA] = ckv.astype(BF16)
    kr = _rope(_norm64(z[:, o:o + LANES], gkr_ref[...], lo), tabb_ref, ROPE_B // 2)
    kr32_ref[...] = kr[:, 0:ROPE_B]
    kcat_ref[:, KV_LORA:KV_LORA + LANES] = kr.astype(BF16)


def _ab_project(x, pos_tiles, gmix, w, taba, tabb):
    t, d = x.shape
    tm = min(256, t)
    n_pos = pos_tiles // tm if pos_tiles >= tm else 1
    tab_spec = pl.BlockSpec((3, tm, LANES), lambda i: (0, i % n_pos, 0))
    row = lambda wdt: pl.BlockSpec((tm, wdt), lambda i: (i, 0))
    out_shapes = (
        jax.ShapeDtypeStruct((t, QA_W), BF16),
        jax.ShapeDtypeStruct((t, KA_W), F32),
        jax.ShapeDtypeStruct((t, VA_W), F32),
        jax.ShapeDtypeStruct((t, KV_LORA), F32),
        jax.ShapeDtypeStruct((t, ROPE_B), F32),
        jax.ShapeDtypeStruct((t, KA_W), BF16),
        jax.ShapeDtypeStruct((t, VA_W), BF16),
        jax.ShapeDtypeStruct((t, QCAT_W), BF16),
        jax.ShapeDtypeStruct((t, H_B * QCAT_W), BF16),
    )
    out_specs = (row(QA_W), row(KA_W), row(VA_W), row(KV_LORA), row(ROPE_B), row(KA_W),
                 row(VA_W), row(QCAT_W), row(H_B * QCAT_W))
    in_specs = [
        row(d), _const_spec((1, d)), _const_spec(w["w_in"].shape), _const_spec(w["w_qup"].shape),
        _const_spec(w["w_uk"].shape), _const_spec((1, LANES)), _const_spec((1, LANES)),
        _const_spec((1, Q_LORA)), _const_spec((1, NOPE_B)), _const_spec((1, LANES)),
        _const_spec((1, KV_LORA)), _const_spec((1, LANES)), tab_spec, tab_spec,
    ]
    return pl.pallas_call(
        _ab_proj_kernel, grid=(t // tm,), in_specs=in_specs, out_specs=out_specs,
        out_shape=out_shapes, compiler_params=_cparams(("parallel",)), name="ab_proj",
    )(x, gmix, w["w_in"], w["w_qup"], w["w_uk"], w["g_q"], w["g_k"], w["g_ql"], w["g_qbn"],
      w["g_qbr"], w["g_kvl"], w["g_kr"], taba, tabb)


def _c_proj_kernel(x_ref, gmix_ref, win_ref, gq_ref, gk_ref, taba_ref,
                   q_ref, k32_ref, v32_ref, kbf_ref, vbf_ref):
    tm = x_ref.shape[0]
    lo = _lo_mask(tm)
    h = _rms(x_ref[...], gmix_ref[...]).astype(BF16)
    z = _dot(h, win_ref[...])
    scale = HD_C ** -0.5
    half = HD_C // ROT_FRAC // 2
    for j in range(QC_W // LANES):
        c = z[:, j * LANES:(j + 1) * LANES]
        r = _rope(_norm64(c, gq_ref[...], lo), taba_ref, half)
        q_ref[:, j * LANES:(j + 1) * LANES] = (r * scale).astype(BF16)
    for j in range(KC_W // LANES):
        c = z[:, QC_W + j * LANES:QC_W + (j + 1) * LANES]
        r = _rope(_norm64(c, gk_ref[...], lo), taba_ref, half)
        k32_ref[:, j * LANES:(j + 1) * LANES] = r
        kbf_ref[:, j * LANES:(j + 1) * LANES] = r.astype(BF16)
    v = z[:, QC_W + KC_W:QC_W + 2 * KC_W]
    v32_ref[...] = v
    vbf_ref[...] = v.astype(BF16)


def _c_project(x, pos_tiles, gmix, w, taba):
    t, d = x.shape
    tm = min(256, t)
    n_pos = pos_tiles // tm if pos_tiles >= tm else 1
    tab_spec = pl.BlockSpec((3, tm, LANES), lambda i: (0, i % n_pos, 0))
    row = lambda wdt: pl.BlockSpec((tm, wdt), lambda i: (i, 0))
    out_shapes = (
        jax.ShapeDtypeStruct((t, QC_W), BF16),
        jax.ShapeDtypeStruct((t, KC_W), F32),
        jax.ShapeDtypeStruct((t, KC_W), F32),
        jax.ShapeDtypeStruct((t, KC_W), BF16),
        jax.ShapeDtypeStruct((t, KC_W), BF16),
    )
    in_specs = [row(d), _const_spec((1, d)), _const_spec(w["w_in"].shape),
                _const_spec((1, LANES)), _const_spec((1, LANES)), tab_spec]
    return pl.pallas_call(
        _c_proj_kernel, grid=(t // tm,), in_specs=in_specs,
        out_specs=(row(QC_W), row(KC_W), row(KC_W), row(KC_W), row(KC_W)),
        out_shape=out_shapes, compiler_params=_cparams(("parallel",)), name="c_proj",
    )(x, gmix, w["w_in"], w["g_q"], w["g_k"], taba)


def _ab_out_kernel(oa_ref, ol_ref, wuv_ref, wo_ref, x_ref, o_ref):
    half = H_A * 2 * DH_A
    acc = x_ref[...] + _dot(oa_ref[...], wo_ref[0:half, :])
    obs = [_dot(ol_ref[:, hh * KV_LORA:(hh + 1) * KV_LORA], wuv_ref[hh]).astype(BF16)
           for hh in range(H_B)]
    ob = jnp.concatenate(obs, axis=1)
    o_ref[...] = acc + _dot(ob, wo_ref[half:, :])


def _ab_out(oa, ol, w, x):
    t, d = x.shape
    tm = min(256, t)
    row = lambda wdt: pl.BlockSpec((tm, wdt), lambda i: (i, 0))
    return pl.pallas_call(
        _ab_out_kernel, grid=(t // tm,),
        in_specs=[row(oa.shape[1]), row(ol.shape[1]), _const_spec(w["w_uv"].shape),
                  _const_spec(w["w_out"].shape), row(d)],
        out_specs=row(d), out_shape=jax.ShapeDtypeStruct((t, d), F32),
        compiler_params=_cparams(("parallel",)), name="ab_out",
    )(oa, ol, w["w_uv"], w["w_out"], x)


def _c_out_kernel(a_ref, wo_ref, x_ref, o_ref):
    o_ref[...] = x_ref[...] + _dot(a_ref[...], wo_ref[...])


def _c_out(a, w_out, x):
    t, d = x.shape
    tm = min(256, t)
    row = lambda wdt: pl.BlockSpec((tm, wdt), lambda i: (i, 0))
    return pl.pallas_call(
        _c_out_kernel, grid=(t // tm,),
        in_specs=[row(a.shape[1]), _const_spec(w_out.shape), row(d)],
        out_specs=row(d), out_shape=jax.ShapeDtypeStruct((t, d), F32),
        compiler_params=_cparams(("parallel",)), name="c_out",
    )(a, w_out, x)


def _ffn_kernel(x_ref, g_ref, wup_ref, wdn_ref, o_ref, h_scr, acc_scr):
    j = pl.program_id(1)

    @pl.when(j == 0)
    def _():
        x = x_ref[...]
        h_scr[...] = _rms(x, g_ref[...]).astype(BF16)
        acc_scr[...] = x

    a = jnp.maximum(_dot(h_scr[...], wup_ref[...]), 0.0)
    acc_scr[...] += _dot((a * a).astype(BF16), wdn_ref[...])

    @pl.when(j == pl.num_programs(1) - 1)
    def _():
        o_ref[...] = acc_scr[...]


def _ffn(x, g, w_up, w_dn):
    t, d = x.shape
    f = w_up.shape[1]
    tm = min(512, t)
    tf = min(1024, f)
    return pl.pallas_call(
        _ffn_kernel, grid=(t // tm, f // tf),
        in_specs=[pl.BlockSpec((tm, d), lambda i, j: (i, 0)), _const_spec((1, d)),
                  pl.BlockSpec((d, tf), lambda i, j: (0, j)),
                  pl.BlockSpec((tf, d), lambda i, j: (j, 0))],
        out_specs=pl.BlockSpec((tm, d), lambda i, j: (i, 0)),
        out_shape=jax.ShapeDtypeStruct((t, d), F32),
        scratch_shapes=[pltpu.VMEM((tm, d), BF16), pltpu.VMEM((tm, d), F32)],
        compiler_params=_cparams(("parallel", "arbitrary")), name="ffn",
    )(x, g, w_up, w_dn)


def _diff_lambda(lam_ref, lam_init):
    l = lam_ref[...]
    a = jnp.sum(l[0:1] * l[1:2], axis=-1, keepdims=True)
    b = jnp.sum(l[2:3] * l[3:4], axis=-1, keepdims=True)
    return jnp.exp(a) - jnp.exp(b) + lam_init


def _lanes(x, width):
    n = width // LANES
    return x if n == 1 else jnp.concatenate([x] * n, axis=1)


def _diff_attn_kernel(lam_ref, gsub_ref, q_ref, k_ref, v_ref, o_ref, m_scr, acc_scr,
                      *, lam_init, tq, tk, gpc):
    qi = pl.program_id(2)
    rows = gpc * tq
    n_ch = G_A // gpc
    m_scr[...] = jnp.full(m_scr.shape, NEG_INF, F32)
    acc_scr[...] = jnp.zeros(acc_scr.shape, F32)

    def tile(ks, width, masked):
        kt = k_ref[pl.ds(ks, width), :]
        vt = v_ref[pl.ds(ks, width), :]
        v1 = jnp.concatenate([vt, jnp.ones_like(vt)], axis=1)
        lo = _lo_mask(width)
        zero = jnp.zeros_like(kt)
        if masked:
            r = lax.broadcasted_iota(jnp.int32, (rows, width), 0) % tq
            cc = lax.broadcasted_iota(jnp.int32, (rows, width), 1)
            keep = cc <= r
        for c in range(2):
            kc = jnp.where(lo, kt, zero) if c == 0 else jnp.where(lo, zero, kt)
            for ch in range(n_ch):
                qs = [q_ref[:, (ch * gpc + u) * LANES:(ch * gpc + u + 1) * LANES] for u in range(gpc)]
                q = qs[0] if gpc == 1 else jnp.concatenate(qs, axis=0)
                s = _dot_nt(q, kc)
                if masked:
                    s = jnp.where(keep, s, NEG_INF)
                m_prev = m_scr[c, ch]
                m_new = jnp.maximum(m_prev, jnp.max(s, axis=-1, keepdims=True))
                alpha = jnp.exp(m_prev - m_new)
                p = jnp.exp(s - _lanes(m_new, width))
                acc_scr[c, ch] = _lanes(alpha, 2 * LANES) * acc_scr[c, ch] + _dot(p.astype(BF16), v1)
                m_scr[c, ch] = m_new

    def body(ki, carry):
        tile(pl.multiple_of(ki * tk, tk), tk, False)
        return carry

    n_full = (qi * tq) // tk
    lax.fori_loop(0, n_full, body, 0)
    if tk > tq:
        @pl.when((qi * tq) % tk != 0)
        def _():
            tile(pl.multiple_of(n_full * tk, tq), tq, False)
    tile(pl.multiple_of(qi * tq, tq), tq, True)

    lam = _diff_lambda(lam_ref, lam_init)
    for ch in range(n_ch):
        a0 = acc_scr[0, ch]
        a1 = acc_scr[1, ch]
        out = a0[:, 0:LANES] / a0[:, LANES:2 * LANES] - lam * (a1[:, 0:LANES] / a1[:, LANES:2 * LANES])
        out = _rms(out, gsub_ref[...]) * (1.0 - lam_init)
        for u in range(gpc):
            g = ch * gpc + u
            o_ref[:, g * LANES:(g + 1) * LANES] = out[u * tq:(u + 1) * tq].astype(BF16)


def _diff_attn(q, k, v, lam_vecs, g_sub, lam_init, batch, seq):
    tq = min(256, seq)
    tk = min(512, seq)
    gpc = 1
    nq = seq // tq
    gw = G_A * LANES
    kern = functools.partial(_diff_attn_kernel, lam_init=lam_init, tq=tq, tk=tk, gpc=gpc)
    return pl.pallas_call(
        kern, grid=(batch, KV_A, nq),
        in_specs=[_const_spec(lam_vecs.shape), _const_spec(g_sub.shape),
                  pl.BlockSpec((tq, gw), lambda b, kv, i: (b * nq + i, kv)),
                  pl.BlockSpec((seq, LANES), lambda b, kv, i: (b, kv)),
                  pl.BlockSpec((seq, LANES), lambda b, kv, i: (b, kv))],
        out_specs=pl.BlockSpec((tq, gw), lambda b, kv, i: (b * nq + i, kv)),
        out_shape=jax.ShapeDtypeStruct(q.shape, BF16),
        scratch_shapes=[pltpu.VMEM((2, G_A // gpc, gpc * tq, LANES), F32),
                        pltpu.VMEM((2, G_A // gpc, gpc * tq, 2 * LANES), F32)],
        compiler_params=_cparams(("parallel", "parallel", "arbitrary")), name="diff_attn",
    )(lam_vecs, g_sub, q, k, v)


def _mla_attn_kernel(q_ref, k_ref, o_ref, m_scr, l_scr, acc_scr, *, tq, tk, hpc):
    qi = pl.program_id(1)
    m_scr[...] = jnp.full(m_scr.shape, NEG_INF, F32)
    l_scr[...] = jnp.zeros(l_scr.shape, F32)
    acc_scr[...] = jnp.zeros(acc_scr.shape, F32)
    rows = hpc * tq

    def tile(ks, width, masked):
        kt = k_ref[pl.ds(ks, width), :]
        vt = kt[:, 0:KV_LORA]
        if masked:
            r = lax.broadcasted_iota(jnp.int32, (rows, width), 0) % tq
            cc = lax.broadcasted_iota(jnp.int32, (rows, width), 1)
            keep = cc <= r
        for ch in range(H_B // hpc):
            qs = [q_ref[:, (ch * hpc + u) * QCAT_W:(ch * hpc + u + 1) * QCAT_W] for u in range(hpc)]
            q = qs[0] if hpc == 1 else jnp.concatenate(qs, axis=0)
            s = _dot_nt(q, kt)
            if masked:
                s = jnp.where(keep, s, NEG_INF)
            m_prev = m_scr[ch]
            m_new = jnp.maximum(m_prev, jnp.max(s, axis=-1, keepdims=True))
            alpha = jnp.exp(m_prev - m_new)
            p = jnp.exp(s - _lanes(m_new, width))
            l_scr[ch] = alpha * l_scr[ch] + jnp.sum(p, axis=-1, keepdims=True)
            acc_scr[ch] = _lanes(alpha, KV_LORA) * acc_scr[ch] + _dot(p.astype(BF16), vt)
            m_scr[ch] = m_new

    def body(ki, carry):
        tile(pl.multiple_of(ki * tk, tk), tk, False)
        return carry

    n_full = (qi * tq) // tk
    lax.fori_loop(0, n_full, body, 0)
    if tk > tq:
        @pl.when((qi * tq) % tk != 0)
        def _():
            tile(pl.multiple_of(n_full * tk, tq), tq, False)
    tile(pl.multiple_of(qi * tq, tq), tq, True)
    for ch in range(H_B // hpc):
        out = acc_scr[ch] / _lanes(l_scr[ch], KV_LORA)
        for u in range(hpc):
            hh = ch * hpc + u
            o_ref[:, hh * KV_LORA:(hh + 1) * KV_LORA] = out[u * tq:(u + 1) * tq].astype(BF16)


def _mla_attn(qcat, kcat, batch, seq):
    tq = min(256, seq)
    tk = min(512, seq)
    nq = seq // tq
    hpc = 4
    kern = functools.partial(_mla_attn_kernel, tq=tq, tk=tk, hpc=hpc)
    return pl.pallas_call(
        kern, grid=(batch, nq),
        in_specs=[pl.BlockSpec((tq, H_B * QCAT_W), lambda b, i: (b * nq + i, 0)),
                  pl.BlockSpec((seq, QCAT_W), lambda b, i: (b, 0))],
        out_specs=pl.BlockSpec((tq, H_B * KV_LORA), lambda b, i: (b * nq + i, 0)),
        out_shape=jax.ShapeDtypeStruct((batch * seq, H_B * KV_LORA), BF16),
        scratch_shapes=[pltpu.VMEM((H_B // hpc, hpc * tq, LANES), F32), pltpu.VMEM((H_B // hpc, hpc * tq, LANES), F32),
                        pltpu.VMEM((H_B // hpc, hpc * tq, KV_LORA), F32)],
        compiler_params=_cparams(("parallel", "arbitrary")), name="mla_attn",
    )(qcat, kcat)


def _swa_attn_kernel(sink_ref, q_ref, kc_ref, kp_ref, vc_ref, vp_ref, o_ref, *, gpc):
    j = pl.program_id(1)
    w = WINDOW
    keys = jnp.concatenate([kp_ref[...], kc_ref[...]], axis=0)
    vals = jnp.concatenate([vp_ref[...], vc_ref[...]], axis=0)
    rows = gpc * w
    r = lax.broadcasted_iota(jnp.int32, (rows, 2 * w), 0) % w
    c = lax.broadcasted_iota(jnp.int32, (rows, 2 * w), 1)
    allowed = (c >= r) & (c <= r + w) & ((j > 0) | (c >= w))
    lo = _lo_mask(2 * w)
    for kvp in range(KV_C // 2):
        kk = keys[:, kvp * LANES:(kvp + 1) * LANES]
        vv = vals[:, kvp * LANES:(kvp + 1) * LANES]
        zero = jnp.zeros_like(kk)
        one = jnp.ones_like(vv)
        ksel = (jnp.where(lo, kk, zero), jnp.where(lo, zero, kk))
        vsel = (jnp.concatenate([jnp.where(lo, vv, zero), one], axis=1),
                jnp.concatenate([jnp.where(lo, zero, vv), one], axis=1))
        for gc in range(G_C // gpc):
            g0 = gc * gpc
            qs = [q_ref[:, (kvp * G_C + g0 + u) * LANES:(kvp * G_C + g0 + u + 1) * LANES]
                  for u in range(gpc)]
            q = qs[0] if gpc == 1 else jnp.concatenate(qs, axis=0)
            out = None
            for par in range(2):
                kv = 2 * kvp + par
                sinks = [jnp.full((w, LANES), sink_ref[kv * G_C + g0 + u], F32) for u in range(gpc)]
                sink = sinks[0] if gpc == 1 else jnp.concatenate(sinks, axis=0)
                s = jnp.where(allowed, _dot_nt(q, ksel[par]), NEG_INF)
                m = jnp.maximum(jnp.max(s, axis=-1, keepdims=True), sink)
                e = jnp.exp(s - _lanes(m, 2 * w))
                acc = _dot(e.astype(BF16), vsel[par])
                o = acc[:, 0:LANES] / (acc[:, LANES:2 * LANES] + jnp.exp(sink - m))
                out = o if out is None else out + o
            for u in range(gpc):
                col = (kvp * G_C + g0 + u) * LANES
                o_ref[:, col:col + LANES] = out[u * w:(u + 1) * w].astype(BF16)


def _swa_attn(q, k, v, sinks, batch, seq):
    nb = seq // WINDOW
    cur = lambda b, j: (b * nb + j, 0)
    prev = lambda b, j: (b * nb + jnp.maximum(j - 1, 0), 0)
    return pl.pallas_call(
        functools.partial(_swa_attn_kernel, gpc=2), grid=(batch, nb),
        in_specs=[pl.BlockSpec(memory_space=pltpu.SMEM),
                  pl.BlockSpec((WINDOW, QC_W), cur),
                  pl.BlockSpec((WINDOW, KC_W), cur), pl.BlockSpec((WINDOW, KC_W), prev),
                  pl.BlockSpec((WINDOW, KC_W), cur), pl.BlockSpec((WINDOW, KC_W), prev)],
        out_specs=pl.BlockSpec((WINDOW, QC_W), cur),
        out_shape=jax.ShapeDtypeStruct(q.shape, BF16),
        compiler_params=_cparams(("parallel", "parallel")), name="swa_attn",
    )(sinks, q, k, k, v, v)


PAGES_PER_CHUNK = 16
DEC_SLOTS = 3


def _dec_ab_kernel(pt_ref, lam_ref, gsub_ref, qz_ref, knew_ref, vnew_ref, qc_ref, cnew_ref,
                   rnew_ref, kc_hbm, vc_hbm, cc_hbm, rc_hbm, oa_ref, ol_ref,
                   kbuf, vbuf, cbuf, rbuf, sem, *, layer, lam_init, n_pages):
    b = pl.program_id(0)
    pp = kbuf.shape[1]
    n_chunks = n_pages // pp
    page_size = kbuf.shape[3]
    streams = ((kc_hbm, kbuf, 0), (vc_hbm, vbuf, KV_A * page_size), (cc_hbm, cbuf, page_size),
               (rc_hbm, rbuf, 0))

    def copies(seq_idx, chunk, slot):
        out = []
        for j in range(pp):
            page = pt_ref[seq_idx, chunk * pp + j]
            for si, (src, dst, flat_rows) in enumerate(streams):
                d = dst.at[slot, pl.ds(j * flat_rows, flat_rows)] if flat_rows else dst.at[slot, j]
                out.append(pltpu.make_async_copy(src.at[layer, page], d, sem.at[si, slot]))
        return out

    def start(seq_idx, chunk, slot):
        for cp in copies(seq_idx, chunk, slot):
            cp.start()

    def wait(seq_idx, chunk, slot):
        for cp in copies(seq_idx, chunk, slot):
            cp.wait()

    qz = qz_ref[0].astype(BF16)
    qc = qc_ref[0]
    ql = qc[:, 0:KV_LORA].astype(BF16)
    qt = qc[:, KV_LORA:QCAT_W].astype(BF16)
    row_kv0 = (lax.broadcasted_iota(jnp.int32, (2 * H_A, LANES), 0) // G_A) % KV_A == 0

    knew = knew_ref[0]
    vnew = vnew_ref[0]
    m_a = jnp.sum(qz_ref[0] * knew, axis=-1, keepdims=True)
    l_a = jnp.ones_like(m_a)
    acc_a = jnp.where(row_kv0, vnew[:, 0:LANES], vnew[:, LANES:2 * LANES])
    cnew = cnew_ref[0]
    rnew = rnew_ref[0]
    m_b = (jnp.sum(qc[:, 0:KV_LORA] * cnew, axis=-1, keepdims=True)
           + jnp.sum(qc[:, KV_LORA:QCAT_W] * rnew, axis=-1, keepdims=True))
    l_b = jnp.ones_like(m_b)
    acc_b = jnp.broadcast_to(cnew, (H_B, KV_LORA))

    ahead = DEC_SLOTS - 1
    total = pl.num_programs(0) * n_chunks

    @pl.when(b == 0)
    def _():
        for g in range(min(ahead, n_chunks)):
            start(0, g, g)

    def body(ci, carry):
        m_a, l_a, acc_a, m_b, l_b, acc_b = carry
        g = b * n_chunks + ci
        nxt = g + ahead

        @pl.when(nxt < total)
        def _():
            start(nxt // n_chunks, nxt % n_chunks, nxt % DEC_SLOTS)

        slot = g % DEC_SLOTS
        wait(b, ci, slot)

        k_all = jnp.concatenate([kbuf[slot, j] for j in range(pp)], axis=1).astype(BF16)
        s = _dot(qz, k_all)
        m_new = jnp.maximum(m_a, jnp.max(s, axis=-1, keepdims=True))
        alpha = jnp.exp(m_a - m_new)
        p = jnp.exp(s - m_new)
        l_a = alpha * l_a + jnp.sum(p, axis=-1, keepdims=True)
        pb = p.astype(BF16)
        v0 = vbuf[slot, pl.ds(0, pp * page_size, stride=KV_A), :].astype(BF16)
        v1 = vbuf[slot, pl.ds(1, pp * page_size, stride=KV_A), :].astype(BF16)
        acc_a = alpha * acc_a + jnp.where(row_kv0, _dot(pb, v0), _dot(pb, v1))
        m_a = m_new

        c_all = cbuf[slot].astype(BF16)
        r_all = jnp.concatenate([rbuf[slot, j] for j in range(pp)], axis=1).astype(BF16)
        s = _dot_nt(ql, c_all) + _dot(qt, jnp.concatenate([r_all, r_all], axis=0))
        m_new = jnp.maximum(m_b, jnp.max(s, axis=-1, keepdims=True))
        alpha = jnp.exp(m_b - m_new)
        p = jnp.exp(s - m_new)
        l_b = alpha * l_b + jnp.sum(p, axis=-1, keepdims=True)
        acc_b = alpha * acc_b + _dot(p.astype(BF16), c_all)
        return m_a, l_a, acc_a, m_new, l_b, acc_b

    m_a, l_a, acc_a, m_b, l_b, acc_b = lax.fori_loop(
        0, n_chunks, body, (m_a, l_a, acc_a, m_b, l_b, acc_b))

    out = acc_a / l_a
    lam = _diff_lambda(lam_ref, lam_init)
    oa = out[0:H_A] - lam * out[H_A:2 * H_A]
    oa_ref[0] = _rms(oa, gsub_ref[...]) * (1.0 - lam_init)
    ol_ref[0] = acc_b / l_b


def _dec_ab_attn(page_table, lam_vecs, g_sub, qz, knew, vnew, qc, cnew, rnew,
                 cache_k, cache_v, cache_c, cache_r, layer, lam_init):
    db, n_pages = page_table.shape
    pp = min(PAGES_PER_CHUNK, n_pages)
    kern = functools.partial(_dec_ab_kernel, layer=layer, lam_init=lam_init, n_pages=n_pages)
    seq_spec = lambda shape: pl.BlockSpec((1,) + shape, lambda b, pt: (b, 0, 0))
    anyspec = pl.BlockSpec(memory_space=pl.ANY)
    grid_spec = pltpu.PrefetchScalarGridSpec(
        num_scalar_prefetch=1, grid=(db,),
        in_specs=[pl.BlockSpec(lam_vecs.shape, lambda b, pt: (0, 0), pipeline_mode=pl.Buffered(1)),
                  pl.BlockSpec(g_sub.shape, lambda b, pt: (0, 0), pipeline_mode=pl.Buffered(1)),
                  seq_spec(qz.shape[1:]), seq_spec(knew.shape[1:]), seq_spec(vnew.shape[1:]),
                  seq_spec(qc.shape[1:]), seq_spec(cnew.shape[1:]), seq_spec(rnew.shape[1:]),
                  anyspec, anyspec, anyspec, anyspec],
        out_specs=[seq_spec((H_A, LANES)), seq_spec((H_B, KV_LORA))],
        scratch_shapes=[pltpu.VMEM((DEC_SLOTS, pp) + cache_k.shape[2:], F32),
                        pltpu.VMEM((DEC_SLOTS, pp * cache_v.shape[2], cache_v.shape[3]), F32),
                        pltpu.VMEM((DEC_SLOTS, pp * cache_c.shape[2], cache_c.shape[3]), F32),
                        pltpu.VMEM((DEC_SLOTS, pp) + cache_r.shape[2:], F32),
                        pltpu.SemaphoreType.DMA((4, DEC_SLOTS))])
    return pl.pallas_call(
        kern, grid_spec=grid_spec,
        out_shape=(jax.ShapeDtypeStruct((db, H_A, LANES), F32),
                   jax.ShapeDtypeStruct((db, H_B, KV_LORA), F32)),
        compiler_params=_cparams(("arbitrary",)), name="dec_ab_attn",
    )(page_table, lam_vecs, g_sub, qz, knew, vnew, qc, cnew, rnew, cache_k, cache_v, cache_c,
      cache_r)


def _dec_c_kernel(sink_ref, qz_ref, kt_ref, vt_ref, knew_ref, vnew_ref, o_ref):
    sink = sink_ref[...]
    for u in range(qz_ref.shape[0]):
        qf = qz_ref[u]
        kt = kt_ref[0, u].astype(BF16)
        vt = vt_ref[0, u].astype(BF16)
        s = _dot(qf.astype(BF16), kt)
        s_new = jnp.sum(qf * knew_ref[u], axis=-1, keepdims=True)
        m = jnp.maximum(jnp.maximum(jnp.max(s, axis=-1, keepdims=True), s_new), sink)
        e = jnp.exp(s - m)
        e_new = jnp.exp(s_new - m)
        denom = jnp.sum(e, axis=-1, keepdims=True) + e_new + jnp.exp(sink - m)
        o = (_dot_nt(e.astype(BF16), vt) + e_new * vnew_ref[u]) / denom
        rkv = lax.broadcasted_iota(jnp.int32, o.shape, 0) // G_C
        ckv = lax.broadcasted_iota(jnp.int32, o.shape, 1) // HD_C
        o_ref[u] = jnp.where(rkv == ckv, o, 0.0)


def _dec_c_attn(sink_col, qz, state_kt, state_vt, knew, vnew, layer):
    db = qz.shape[0]
    w = state_kt.shape[-1]
    sb = 8 if db % 8 == 0 else 1
    seq_spec = lambda shape: pl.BlockSpec((sb,) + shape, lambda b: (b, 0, 0))
    st_spec = pl.BlockSpec((1, sb, KC_W, w), lambda b: (layer, b, 0, 0))
    return pl.pallas_call(
        _dec_c_kernel, grid=(db // sb,),
        in_specs=[_const_spec(sink_col.shape), seq_spec(qz.shape[1:]), st_spec, st_spec,
                  seq_spec(knew.shape[1:]), seq_spec(vnew.shape[1:])],
        out_specs=seq_spec((H_C, KC_W)),
        out_shape=jax.ShapeDtypeStruct((db, H_C, KC_W), F32),
        compiler_params=_cparams(("parallel",)), name="dec_c_attn",
    )(sink_col, qz, state_kt, state_vt, knew, vnew)


def _rope_tables(pos, rot):
    half = rot // 2
    inv = ROPE_THETA ** (-jnp.arange(half, dtype=F32) / half)
    ang = pos.astype(F32)[:, None] * inv[None, :]
    cos, sin = jnp.cos(ang), jnp.sin(ang)
    jj = jnp.arange(LANES) % 64
    idx = jj % half
    c = jnp.where(jj < rot, cos[:, idx], 1.0)
    s1 = jnp.where(jj < half, -sin[:, idx], 0.0)
    s2 = jnp.where((jj >= half) & (jj < rot), sin[:, idx], 0.0)
    return jnp.stack([c, s1, s2]).astype(F32)


def _tile2(g):
    return jnp.concatenate([g, g]).reshape(1, -1).astype(F32)


def _prep_ab(i, w_in_ab, g_q_a, g_k_a, g_qlat, w_q_up, g_q_b, g_kvlat, g_kr_b, w_uk, w_uv,
             w_out_ab):
    w_in = w_in_ab[i]
    kr_cols = w_in[:, -ROPE_B:]
    w_in = jnp.concatenate([w_in, kr_cols], axis=1).astype(BF16)
    wq = w_q_up[i].reshape(Q_LORA, H_B, NOPE_B + ROPE_B)
    w_qup = jnp.concatenate([wq[:, :, :NOPE_B].reshape(Q_LORA, H_B * NOPE_B),
                             wq[:, :, NOPE_B:].reshape(Q_LORA, H_B * ROPE_B)], axis=1).astype(BF16)
    return {
        "w_in": w_in, "w_qup": w_qup,
        "w_uk": jnp.transpose(w_uk[i], (1, 2, 0)).astype(BF16),
        "w_uv": jnp.transpose(w_uv[i], (1, 0, 2)).astype(BF16),
        "w_out": w_out_ab[i].astype(BF16),
        "g_q": _tile2(g_q_a[i]), "g_k": _tile2(g_k_a[i]),
        "g_ql": g_qlat[i].reshape(1, -1), "g_qbn": g_q_b[i, :NOPE_B].reshape(1, -1),
        "g_qbr": _tile2(g_q_b[i, NOPE_B:]), "g_kvl": g_kvlat[i].reshape(1, -1),
        "g_kr": _tile2(g_kr_b[i]),
    }


def _prep_c(i, w_in_c, g_q_c, g_k_c, w_out_c):
    d = w_in_c.shape[1]
    w_in = w_in_c[i]
    wq = w_in[:, :QC_W].reshape(d, KV_C // 2, 2, G_C, HD_C).transpose(0, 1, 3, 2, 4).reshape(d, QC_W)
    w_in = jnp.concatenate([wq, w_in[:, QC_W:]], axis=1).astype(BF16)
    wo = w_out_c[i].reshape(KV_C // 2, 2, G_C, HD_C, -1).transpose(0, 2, 1, 3, 4).reshape(QC_W, -1)
    return {"w_in": w_in, "w_out": wo.astype(BF16), "g_q": _tile2(g_q_c[i]), "g_k": _tile2(g_k_c[i])}


def kernel(x_prompt, x_sample, cache_diff_k, cache_diff_v, cache_mla_ckv, cache_mla_krope,
           state_swa_k, state_swa_v, page_table, g_mix, g_ffn, w_in_ab, g_q_a, g_k_a, lam_a,
           g_sub_a, g_qlat, w_q_up, g_q_b, g_kvlat, g_kr_b, w_uk, w_uv, w_out_ab, w_in_c,
           g_q_c, g_k_c, sinks_c, w_out_c, w_up, w_down):
    batch, seq, d = x_prompt.shape
    db = x_sample.shape[0]
    depth = g_mix.shape[0]
    n_pool, page_size = cache_diff_k.shape[1], cache_diff_k.shape[2]
    n_ab = cache_diff_k.shape[0]
    n_c = state_swa_k.shape[0]
    past_len = page_table.shape[1] * page_size
    t_p = batch * seq

    yp = x_prompt.reshape(t_p, d)
    ys = x_sample.reshape(db, d)

    pos_p = jnp.arange(seq)
    pos_s = jnp.full((db,), past_len)
    taba_p, tabb_p = _rope_tables(pos_p, DH_A // ROT_FRAC), _rope_tables(pos_p, ROPE_B)
    taba_s, tabb_s = _rope_tables(pos_s, DH_A // ROT_FRAC), _rope_tables(pos_s, ROPE_B)

    ck = jnp.transpose(cache_diff_k, (0, 1, 3, 4, 5, 2)).reshape(n_ab, n_pool, KA_W, page_size)
    cv = cache_diff_v.reshape(n_ab, n_pool, page_size * KV_A, 2 * DH_A)
    cr = jnp.transpose(cache_mla_krope, (0, 1, 3, 2))
    skt = jnp.transpose(state_swa_k, (0, 1, 3, 4, 2)).reshape(n_c, db, KC_W, WINDOW)
    svt = jnp.transpose(state_swa_v, (0, 1, 3, 4, 2)).reshape(n_c, db, KC_W, WINDOW)

    eye_kv = jnp.eye(KV_A, dtype=F32)
    eye_c = jnp.eye(2, dtype=F32)
    eye_kvc = jnp.eye(KV_C, dtype=F32)

    outs = {k: [] for k in ("dk_p", "dv_p", "ck_p", "kr_p", "sk_p", "sv_p",
                            "dk_s", "dv_s", "ck_s", "kr_s", "sk_s", "sv_s")}
    for l in range(depth):
        i = l // 2
        gm = g_mix[l].reshape(1, d)
        if l % 2 == 0:
            lam_init = 0.8 - 0.6 * math.exp(-0.3 * l)
            w = _prep_ab(i, w_in_ab, g_q_a, g_k_a, g_qlat, w_q_up, g_q_b, g_kvlat, g_kr_b, w_uk,
                         w_uv, w_out_ab)
            lam_vecs = lam_a[i]
            g_sub = g_sub_a[i].reshape(1, -1)
            q, k32, v32, c32, r32, kbf, vbf, kcat, qcat = _ab_project(yp, seq, gm, w, taba_p, tabb_p)
            oa = _diff_attn(q, kbf, vbf, lam_vecs, g_sub, lam_init, batch, seq)
            ol = _mla_attn(qcat, kcat, batch, seq)
            yp = _ab_out(oa, ol, w, yp)
            outs["dk_p"].append(k32.reshape(batch, seq, KV_A, 2, DH_A))
            outs["dv_p"].append(v32.reshape(batch, seq, KV_A, 2 * DH_A))
            outs["ck_p"].append(c32.reshape(batch, seq, KV_LORA))
            outs["kr_p"].append(r32.reshape(batch, seq, ROPE_B))
            q, k32, v32, c32, r32, _, _, _, qcat = _ab_project(ys, db, gm, w, taba_s, tabb_s)
            q5 = q.astype(F32).reshape(db, KV_A, G_A, 2, DH_A)
            qz = jnp.einsum("bkgcd,kK,cC->bckgKCd", q5, eye_kv, eye_c).reshape(db, 2 * H_A, KA_W)
            oa, ol = _dec_ab_attn(
                page_table, lam_vecs, g_sub, qz, k32.reshape(db, 1, KA_W), v32.reshape(db, 1, VA_W),
                qcat.astype(F32).reshape(db, H_B, QCAT_W), c32.reshape(db, 1, KV_LORA),
                jnp.concatenate([r32, r32], axis=1).reshape(db, 1, LANES),
                ck, cv, cache_mla_ckv, cr, i, lam_init)
            ys = _ab_out(oa.reshape(db, QA_W).astype(BF16),
                         ol.reshape(db, H_B * KV_LORA).astype(BF16), w, ys)
            outs["dk_s"].append(k32.reshape(db, 1, KV_A, 2, DH_A))
            outs["dv_s"].append(v32.reshape(db, 1, KV_A, 2 * DH_A))
            outs["ck_s"].append(c32.reshape(db, 1, KV_LORA))
            outs["kr_s"].append(r32.reshape(db, 1, ROPE_B))
        else:
            w = _prep_c(i, w_in_c, g_q_c, g_k_c, w_out_c)
            q, k32, v32, kbf, vbf = _c_project(yp, seq, gm, w, taba_p)
            a = _swa_attn(q, kbf, vbf, sinks_c[i], batch, seq)
            yp = _c_out(a, w["w_out"], yp)
            outs["sk_p"].append(k32.reshape(batch, seq, KV_C, HD_C)[:, -WINDOW:])
            outs["sv_p"].append(v32.reshape(batch, seq, KV_C, HD_C)[:, -WINDOW:])
            q, k32, v32, _, _ = _c_project(ys, db, gm, w, taba_s)
            q5 = q.astype(F32).reshape(db, KV_C // 2, G_C, 2, HD_C)
            q4 = q5.transpose(0, 1, 3, 2, 4).reshape(db, KV_C, G_C, HD_C)
            qz = jnp.einsum("bkgd,kK->bkgKd", q4, eye_kvc).reshape(db, H_C, KC_W)
            o = _dec_c_attn(sinks_c[i].reshape(H_C, 1), qz, skt, svt, k32.reshape(db, 1, KC_W),
                            v32.reshape(db, 1, KC_W), i)
            o = o.reshape(db, KV_C, G_C, KV_C, HD_C).sum(axis=3)
            a = o.reshape(db, KV_C // 2, 2, G_C, HD_C).transpose(0, 1, 3, 2, 4).reshape(db, QC_W)
            ys = _c_out(a.astype(BF16), w["w_out"], ys)
            knew = k32.reshape(db, 1, KV_C, HD_C)
            vnew = v32.reshape(db, 1, KV_C, HD_C)
            outs["sk_s"].append(jnp.concatenate([state_swa_k[i], knew], axis=1)[:, -WINDOW:])
            outs["sv_s"].append(jnp.concatenate([state_swa_v[i], vnew], axis=1)[:, -WINDOW:])
        gf = g_ffn[l].reshape(1, d)
        wu = w_up[l].astype(BF16)
        wd = w_down[l].astype(BF16)
        yp = _ffn(yp, gf, wu, wd)
        ys = _ffn(ys, gf, wu, wd)

    st = lambda name: jnp.stack(outs[name])
    return (yp.reshape(batch, seq, d), ys.reshape(db, 1, d),
            st("dk_p"), st("dv_p"), st("ck_p"), st("kr_p"), st("sk_p"), st("sv_p"),
            st("dk_s"), st("dv_s"), st("ck_s"), st("kr_s"), st("sk_s"), st("sv_s"))
```
